```python
import math
import jax, jax.numpy as jnp
from jax import lax
import numpy as np

D_MODEL = 1024
BATCH = 4
SEQ = 4096
DEPTH = 1
DEC_BATCH = 32
DEC_SEQ = 1
PAST_LEN = 16384
PAGE_SIZE = 128

RET_DK = 256
RET_HEADS = D_MODEL // RET_DK
RET_DV = 2 * RET_DK
RET_QK = RET_HEADS * RET_DK
RET_V = RET_HEADS * RET_DV
RET_CHUNK = 128
ROPE_BASE = 10000.0
ATT_GROUPS = ((128, 1), (512, 4), (2048, 16))
N_GROUPS = len(ATT_GROUPS)
ATT_HPG = 4
ATT_HD = 128
ATT_HEADS = N_GROUPS * ATT_HPG
ATT_W = ATT_HEADS * ATT_HD
ATT_OUT = ATT_HPG * ATT_HD
ATT_STREAM_WIN = 128
FFN_HIDDEN = -(-8 * D_MODEL // (3 * 256)) * 256
EPS = 1e-6
IN_WIDTHS = (RET_QK, RET_QK, RET_V, RET_V, ATT_W, ATT_W, ATT_W, D_MODEL, D_MODEL)
IN_TOTAL = sum(IN_WIDTHS)
IN_SPLIT_IDX = tuple(int(c) for c in np.cumsum(IN_WIDTHS)[:-1])

kernel_name = "retnet_longnet_gated_hybrid"


def rmsnorm(x, g):
    xf = x.astype(jnp.float32)
    y = xf * lax.rsqrt(jnp.mean(xf * xf, axis=-1, keepdims=True) + EPS)
    return (y * g.astype(jnp.float32)).astype(x.dtype)


def rotary(x, pos):
    half = x.shape[-1] // 2
    inv = ROPE_BASE ** (-jnp.arange(half, dtype=jnp.float32) / half)
    ang = pos.astype(jnp.float32)[:, None] * inv[None, :]
    cos = jnp.cos(ang)[None, :, None, :]
    sin = jnp.sin(ang)[None, :, None, :]
    x1, x2 = x[..., :half], x[..., half:]
    return jnp.concatenate([x1 * cos - x2 * sin, x1 * sin + x2 * cos], axis=-1)


def retention(q, k, v, s0, chunk):
    B, T, H, dk = q.shape
    dv = v.shape[-1]
    n = T // chunk
    lg = jnp.log(1.0 - 2.0 ** (-5.0 - jnp.arange(H, dtype=jnp.float32)))
    i = jnp.arange(chunk, dtype=jnp.float32)
    diff = i[:, None] - i[None, :]
    d_intra = jnp.where(diff >= 0, jnp.exp(lg[:, None, None] * jnp.maximum(diff, 0.0)), 0.0)
    q_dec = jnp.exp(lg[:, None] * (i + 1.0))[:, :, None]
    k_dec = jnp.exp(lg[:, None] * (chunk - 1.0 - i))[:, :, None]
    c_dec = jnp.exp(lg * chunk)[:, None, None]

    def to_chunks(a):
        return a.reshape(B, n, chunk, H, a.shape[-1]).transpose(1, 0, 3, 2, 4)

    def step(S, xs):
        qc, kc, vc = xs
        sc = jnp.einsum("bhid,bhjd->bhij", qc, kc) * d_intra
        o = jnp.einsum("bhij,bhjv->bhiv", sc, vc) + jnp.einsum("bhid,bhdv->bhiv", qc * q_dec, S)
        S_new = S * c_dec + jnp.einsum("bhjd,bhjv->bhdv", kc * k_dec, vc)
        return S_new, o

    s_fin, o = lax.scan(step, s0, (to_chunks(q), to_chunks(k), to_chunks(v)))
    o = o.transpose(1, 0, 3, 2, 4).reshape(B, T, H, dv)
    return o, s_fin


def dilated_prompt(q, k, v, dil):
    B, T, H, E = q.shape
    M = ATT_STREAM_WIN
    L = T // dil
    nb = -(-L // M)
    Lp = nb * M

    def stream(a):
        a = a.reshape(B, L, dil, H, E).transpose(0, 2, 1, 3, 4)
        return jnp.pad(a, ((0, 0), (0, 0), (0, Lp - L), (0, 0), (0, 0)))

    def band(a):
        a = jnp.pad(stream(a), ((0, 0), (0, 0), (M, 0), (0, 0), (0, 0)))
        prev = a[:, :, :Lp].reshape(B, dil, nb, M, H, E)
        cur = a[:, :, M:].reshape(B, dil, nb, M, H, E)
        return jnp.concatenate([prev, cur], axis=3)

    qs = stream(q).reshape(B, dil, nb, M, H, E)
    kb, vb = band(k), band(v)
    s = jnp.einsum("bdnihe,bdnjhe->bdnhij", qs, kb) * (E ** -0.5)
    ii = jnp.arange(M)[None, :, None]
    jj = jnp.arange(2 * M)[None, None, :]
    blk = jnp.arange(nb)[:, None, None]
    off = ii + M - jj
    valid = (off >= 0) & (off <= M) & (blk * M - M + jj >= 0)
    s = jnp.where(valid[None, None, :, None, :, :], s, -jnp.inf)
    m = jnp.max(s, axis=-1, keepdims=True)
    p = jnp.exp(s - m)
    l = jnp.sum(p, axis=-1)
    o = jnp.einsum("bdnhij,bdnjhe->bdnihe", p, vb) / jnp.transpose(l, (0, 1, 2, 4, 3))[..., None]
    lse = jnp.transpose(m[..., 0] + jnp.log(l), (0, 1, 2, 4, 3))

    def unstream(a):
        a = a.reshape((B, dil, Lp) + a.shape[4:])[:, :, :L]
        return jnp.swapaxes(a, 1, 2).reshape((B, T) + a.shape[3:])

    return unstream(o), unstream(lse)


def dilated_sample(q, k, v, kbuf, vbuf, dil):
    Bd, S, H, E = q.shape
    M = ATT_STREAM_WIN
    Wb = kbuf.shape[1]
    kc = jnp.concatenate([kbuf, k], axis=1)
    vc = jnp.concatenate([vbuf, v], axis=1)
    idx = Wb + jnp.arange(S)[:, None] - dil * jnp.arange(M + 1)[None, :]
    valid = idx >= 0
    idx = jnp.maximum(idx, 0)
    kg, vg = kc[:, idx], vc[:, idx]
    s = jnp.einsum("bshe,bsmhe->bshm", q, kg) * (E ** -0.5)
    s = jnp.where(valid[None, :, None, :], s, -jnp.inf)
    m = jnp.max(s, axis=-1, keepdims=True)
    p = jnp.exp(s - m)
    l = jnp.sum(p, axis=-1)
    o = jnp.einsum("bshm,bsmhe->bshe", p, vg) / l[..., None]
    return o, m[..., 0] + jnp.log(l)


def token_mixer(xn, pos, ret_s0, kv_bufs, w_in, ret_gn_g, w_pa, w_pb, w_o):
    B, T, _ = xn.shape
    f32 = jnp.float32
    proj = xn @ w_in
    q_r, k_r, v_r, g_r, q_a, k_a, v_a, gate_a, gate_b = jnp.split(proj, IN_SPLIT_IDX, axis=-1)
    qr = rotary(q_r.reshape(B, T, RET_HEADS, RET_DK).astype(f32), pos)
    kr = rotary(k_r.reshape(B, T, RET_HEADS, RET_DK).astype(f32), pos) * (RET_DK ** -0.5)
    vr = v_r.reshape(B, T, RET_HEADS, RET_DV).astype(f32)
    s0 = jnp.zeros((B, RET_HEADS, RET_DK, RET_DV), f32) if ret_s0 is None else ret_s0.astype(f32)
    chunk = T if T <= RET_CHUNK else math.gcd(T, RET_CHUNK)
    o_r, s_new = retention(qr, kr, vr, s0, chunk)
    o_r = o_r * lax.rsqrt(jnp.mean(o_r * o_r, axis=-1, keepdims=True) + EPS)
    o_r = o_r.reshape(B, T, RET_V) * ret_gn_g.astype(f32) * jax.nn.silu(g_r.astype(f32))
    branch_a = o_r.astype(xn.dtype) @ w_pa
    qa = q_a.reshape(B, T, ATT_HEADS, ATT_HD)
    ka = k_a.reshape(B, T, ATT_HEADS, ATT_HD)
    va = v_a.reshape(B, T, ATT_HEADS, ATT_HD)
    outs, lses, new_kv = [], [], []
    for gi, (win, dil) in enumerate(ATT_GROUPS):
        hs = slice(gi * ATT_HPG, (gi + 1) * ATT_HPG)
        qg, kg, vg = qa[:, :, hs], ka[:, :, hs], va[:, :, hs]
        kv_rows = jnp.stack([kg, vg], axis=2)
        if kv_bufs is None:
            o, lse = dilated_prompt(qg.astype(f32), kg.astype(f32), vg.astype(f32), dil)
            new_kv.append(kv_rows[:, T - min(win, T):])
        else:
            buf = kv_bufs[gi].astype(f32)
            o, lse = dilated_sample(qg.astype(f32), kg.astype(f32), vg.astype(f32),
                                    buf[:, :, 0], buf[:, :, 1], dil)
            new_kv.append(kv_rows)
        outs.append(o)
        lses.append(lse)
    alpha = jax.nn.softmax(jnp.stack(lses, axis=0), axis=0)[..., None]
    o_a = jnp.sum(alpha * jnp.stack(outs, axis=0), axis=0).reshape(B, T, ATT_OUT)
    branch_b = o_a.astype(xn.dtype) @ w_pb
    merged = jax.nn.sigmoid(gate_a) * branch_a + jax.nn.sigmoid(gate_b) * branch_b
    return merged @ w_o, s_new, new_kv


def swiglu(xn, w_ffn_in, w_ffn_out):
    a, b = jnp.split(xn @ w_ffn_in, 2, axis=-1)
    return (jax.nn.silu(a) * b) @ w_ffn_out


def run_trunk(x, pos, ret_states, kv_caches, ln1_g, w_in, ret_gn_g, w_pa, w_pb, w_o,
              ln2_g, w_ffn_in, w_ffn_out, lnf_g):
    h = x
    new_s = []
    new_kv = [[] for _ in range(N_GROUPS)]
    for layer in range(DEPTH):
        s0 = None if ret_states is None else ret_states[layer]
        bufs = None if kv_caches is None else [c[layer] for c in kv_caches]
        mix, s_new, kvs = token_mixer(rmsnorm(h, ln1_g[layer]), pos, s0, bufs, w_in[layer],
                                      ret_gn_g[layer], w_pa[layer], w_pb[layer], w_o[layer])
        h = h + mix
        h = h + swiglu(rmsnorm(h, ln2_g[layer]), w_ffn_in[layer], w_ffn_out[layer])
        new_s.append(s_new.astype(x.dtype))
        for gi in range(N_GROUPS):
            new_kv[gi].append(kvs[gi])
    y = rmsnorm(h, lnf_g)
    return y, jnp.stack(new_s, axis=0), [jnp.stack(a, axis=0) for a in new_kv]


def setup_inputs(seed: int = 0) -> dict:
    key = jax.random.key(seed)
    ks = jax.random.split(key, 16)
    nrm = jax.random.normal
    f32 = jnp.float32
    return {
        "x_prompt": nrm(ks[0], (BATCH, SEQ, D_MODEL), f32),
        "x_sample": nrm(ks[1], (DEC_BATCH, DEC_SEQ, D_MODEL), f32),
        "state_ret": 0.1 * nrm(ks[2], (DEPTH, DEC_BATCH, RET_HEADS, RET_DK, RET_DV), f32),
        "cache_kv_w128": nrm(ks[3], (DEPTH, DEC_BATCH, min(ATT_GROUPS[0][0], PAST_LEN), 2, ATT_HPG, ATT_HD), f32),
        "cache_kv_w512": nrm(ks[4], (DEPTH, DEC_BATCH, min(ATT_GROUPS[1][0], PAST_LEN), 2, ATT_HPG, ATT_HD), f32),
        "cache_kv_w2048": nrm(ks[5], (DEPTH, DEC_BATCH, min(ATT_GROUPS[2][0], PAST_LEN), 2, ATT_HPG, ATT_HD), f32),
        "ln1_g": 1.0 + 0.02 * nrm(ks[6], (DEPTH, D_MODEL), f32),
        "w_in": nrm(ks[7], (DEPTH, D_MODEL, IN_TOTAL), f32) * D_MODEL ** -0.5,
        "ret_gn_g": 1.0 + 0.02 * nrm(ks[8], (DEPTH, RET_V), f32),
        "w_pa": nrm(ks[9], (DEPTH, RET_V, D_MODEL), f32) * RET_V ** -0.5,
        "w_pb": nrm(ks[10], (DEPTH, ATT_OUT, D_MODEL), f32) * ATT_OUT ** -0.5,
        "w_o": nrm(ks[11], (DEPTH, D_MODEL, D_MODEL), f32) * D_MODEL ** -0.5,
        "ln2_g": 1.0 + 0.02 * nrm(ks[12], (DEPTH, D_MODEL), f32),
        "w_ffn_in": nrm(ks[13], (DEPTH, D_MODEL, 2 * FFN_HIDDEN), f32) * D_MODEL ** -0.5,
        "w_ffn_out": nrm(ks[14], (DEPTH, FFN_HIDDEN, D_MODEL), f32) * FFN_HIDDEN ** -0.5,
        "lnf_g": 1.0 + 0.02 * nrm(ks[15], (D_MODEL,), f32),
    }


def reference(x_prompt, x_sample, state_ret, cache_kv_w128, cache_kv_w512, cache_kv_w2048,
              ln1_g, w_in, ret_gn_g, w_pa, w_pb, w_o, ln2_g, w_ffn_in, w_ffn_out, lnf_g):
    pos_p = jnp.arange(x_prompt.shape[1], dtype=jnp.int32)
    pos_s = PAST_LEN + jnp.arange(x_sample.shape[1], dtype=jnp.int32)
    y_prompt, s_p, kv_p = run_trunk(x_prompt, pos_p, None, None, ln1_g, w_in, ret_gn_g, w_pa, w_pb,
                                    w_o, ln2_g, w_ffn_in, w_ffn_out, lnf_g)
    y_sample, s_s, kv_s = run_trunk(x_sample, pos_s, state_ret,
                                    [cache_kv_w128, cache_kv_w512, cache_kv_w2048],
                                    ln1_g, w_in, ret_gn_g, w_pa, w_pb, w_o, ln2_g,
                                    w_ffn_in, w_ffn_out, lnf_g)
    return (y_prompt, y_sample, s_p, s_s, kv_p[0], kv_s[0], kv_p[1], kv_s[1], kv_p[2], kv_s[2])
```

```python
import functools

import jax
import jax.numpy as jnp
from jax import lax
from jax.experimental import pallas as pl
from jax.experimental.pallas import tpu as pltpu

F32 = jnp.float32
BF16 = jnp.bfloat16

D_MODEL = 1024
PAST_LEN = 16384
RET_DK = 256
RET_HEADS = D_MODEL // RET_DK
RET_DV = 2 * RET_DK
RET_QK = RET_HEADS * RET_DK
RET_V = RET_HEADS * RET_DV
ROPE_BASE = 10000.0
ATT_GROUPS = ((128, 1), (512, 4), (2048, 16))
N_GROUPS = len(ATT_GROUPS)
ATT_HPG = 4
ATT_HD = 128
ATT_W = N_GROUPS * ATT_HPG * ATT_HD
ATT_OUT = ATT_HPG * ATT_HD
ATT_STREAM_WIN = 128
FFN_HIDDEN = -(-8 * D_MODEL // (3 * 256)) * 256
EPS = 1e-6
ATT_SCALE = ATT_HD ** -0.5
MASK_VALUE = -1e30

OFF_QR = 0
OFF_KR = OFF_QR + RET_QK
OFF_VR = OFF_KR + RET_QK
OFF_GR = OFF_VR + RET_V
OFF_QA = OFF_GR + RET_V
OFF_KA = OFF_QA + ATT_W
OFF_VA = OFF_KA + ATT_W
OFF_GATE = OFF_VA + ATT_W

VMEM_LIMIT_BYTES = 56 * 1024 * 1024

RET_CHUNK = 256
RET_ROWS_PER_STEP = 1024
PROJ_TN = 512


def _params(*sem):
    return pltpu.CompilerParams(dimension_semantics=sem, vmem_limit_bytes=VMEM_LIMIT_BYTES)


def _resident(shape):
    zeros = (0,) * len(shape)
    return pl.BlockSpec(shape, lambda *_: zeros, pipeline_mode=pl.Buffered(1))


def _sigmoid(x):
    return 1.0 / (1.0 + jnp.exp(-x))


def _rms(x, g):
    return x * lax.rsqrt(jnp.mean(x * x, axis=-1, keepdims=True) + EPS) * g


def _rmsnorm_kernel(x_ref, g_ref, o_ref):
    o_ref[...] = _rms(x_ref[...], g_ref[...]).astype(o_ref.dtype)


def _rmsnorm(x, g, tm):
    m, d = x.shape
    return pl.pallas_call(
        _rmsnorm_kernel,
        out_shape=jax.ShapeDtypeStruct((m, d), BF16),
        grid=(m // tm,),
        in_specs=[pl.BlockSpec((tm, d), lambda i: (i, 0)),
                  pl.BlockSpec((1, d), lambda i: (0, 0))],
        out_specs=pl.BlockSpec((tm, d), lambda i: (i, 0)),
        compiler_params=_params("parallel"),
        name="rmsnorm",
    )(x, g)


def _proj_kernel(a_ref, w_ref, o_ref):
    o_ref[...] = jnp.dot(a_ref[...], w_ref[...], preferred_element_type=F32).astype(o_ref.dtype)


def _proj_rot_kernel(a_ref, w_ref, cos_ref, sin_ref, o_ref, *, scale):
    acc = jnp.dot(a_ref[...], w_ref[...], preferred_element_type=F32)
    cos = cos_ref[...]
    sin = sin_ref[...]
    half = RET_DK // 2
    for h in range(acc.shape[1] // RET_DK):
        x1 = acc[:, h * RET_DK:h * RET_DK + half]
        x2 = acc[:, h * RET_DK + half:(h + 1) * RET_DK]
        o_ref[:, h * RET_DK:h * RET_DK + half] = ((x1 * cos - x2 * sin) * scale).astype(o_ref.dtype)
        o_ref[:, h * RET_DK + half:(h + 1) * RET_DK] = ((x1 * sin + x2 * cos) * scale).astype(o_ref.dtype)


def _proj(xn, w_in, col_off, width, out_dtype, tm, rot=None, scale=1.0):
    m, k = xn.shape
    tn = PROJ_TN
    off = col_off // tn
    in_specs = [pl.BlockSpec((tm, k), lambda i, j: (i, 0)),
                pl.BlockSpec((k, tn), lambda i, j: (0, j + off))]
    args = [xn, w_in]
    if rot is None:
        body = _proj_kernel
    else:
        cos, sin = rot
        npos = cos.shape[0] // tm
        in_specs += [pl.BlockSpec((tm, RET_DK // 2), lambda i, j: (i % npos, 0))] * 2
        args += [cos, sin]
        body = functools.partial(_proj_rot_kernel, scale=scale)
    return pl.pallas_call(
        body,
        out_shape=jax.ShapeDtypeStruct((m, width), out_dtype),
        grid=(m // tm, width // tn),
        in_specs=in_specs,
        out_specs=pl.BlockSpec((tm, tn), lambda i, j: (i, j)),
        compiler_params=_params("parallel", "arbitrary"),
        name="in_proj",
    )(*args)


def _ret_prompt_kernel(q_ref, k_ref, v_ref, g_ref, gn_ref, dint_ref, qdec_ref, kdec_ref, cdec_ref,
                       o_ref, s_ref, *, chunk, nchunks):
    @pl.when(pl.program_id(2) == 0)
    def _():
        s_ref[...] = jnp.zeros_like(s_ref)

    dint = dint_ref[0]
    qdec = qdec_ref[0]
    kdec = kdec_ref[0]
    cdec = cdec_ref[0]
    gn = gn_ref[...]

    def body(ci, carry):
        rows = pl.ds(pl.multiple_of(ci * chunk, chunk), chunk)
        q = q_ref[rows, :]
        k = k_ref[rows, :]
        v = v_ref[rows, :]
        s_old = s_ref[0, 0]
        sc = lax.dot_general(q, k, (((1,), (1,)), ((), ())), preferred_element_type=F32) * dint
        o = jnp.dot(sc.astype(BF16), v, preferred_element_type=F32)
        o = o + qdec * jnp.dot(q, s_old.astype(BF16), preferred_element_type=F32)
        kd_t = (k.astype(F32) * kdec).T.astype(BF16)
        s_ref[0, 0] = s_old * cdec + jnp.dot(kd_t, v, preferred_element_type=F32)
        on = o * lax.rsqrt(jnp.mean(o * o, axis=-1, keepdims=True) + EPS)
        g = g_ref[rows, :].astype(F32)
        o_ref[rows, :] = (on * gn * (g * _sigmoid(g))).astype(o_ref.dtype)
        return carry

    lax.fori_loop(0, nchunks, body, 0)


def _ret_tables(chunk):
    lg = jnp.log(1.0 - 2.0 ** (-5.0 - jnp.arange(RET_HEADS, dtype=F32)))
    i = jnp.arange(chunk, dtype=F32)
    diff = i[:, None] - i[None, :]
    d_intra = jnp.where(diff >= 0, jnp.exp(lg[:, None, None] * jnp.maximum(diff, 0.0)), 0.0)
    q_dec = jnp.exp(lg[:, None] * (i + 1.0))[:, :, None]
    k_dec = jnp.exp(lg[:, None] * (chunk - 1.0 - i))[:, :, None]
    c_dec = jnp.exp(lg * chunk)[:, None, None]
    return (d_intra,
            jnp.broadcast_to(q_dec, (RET_HEADS, chunk, RET_DV)),
            jnp.broadcast_to(k_dec, (RET_HEADS, chunk, RET_DK)),
            jnp.broadcast_to(c_dec, (RET_HEADS, 1, RET_DV)))


def _ret_prompt(q, k, v, g, gn, batch, seq):
    rows = RET_ROWS_PER_STEP
    chunk = RET_CHUNK
    nt = seq // rows
    d_intra, q_dec, k_dec, c_dec = _ret_tables(chunk)
    row_map = lambda b, h, c: (b * nt + c, h)
    head_map = lambda b, h, c: (h, 0, 0)
    return pl.pallas_call(
        functools.partial(_ret_prompt_kernel, chunk=chunk, nchunks=rows // chunk),
        out_shape=(jax.ShapeDtypeStruct((batch * seq, RET_V), BF16),
                   jax.ShapeDtypeStruct((batch, RET_HEADS, RET_DK, RET_DV), F32)),
        grid=(batch, RET_HEADS, nt),
        in_specs=[pl.BlockSpec((rows, RET_DK), row_map),
                  pl.BlockSpec((rows, RET_DK), row_map),
                  pl.BlockSpec((rows, RET_DV), row_map),
                  pl.BlockSpec((rows, RET_DV), row_map),
                  pl.BlockSpec((1, RET_DV), lambda b, h, c: (0, h)),
                  pl.BlockSpec((1, chunk, chunk), head_map),
                  pl.BlockSpec((1, chunk, RET_DV), head_map),
                  pl.BlockSpec((1, chunk, RET_DK), head_map),
                  pl.BlockSpec((1, 1, RET_DV), head_map)],
        out_specs=(pl.BlockSpec((rows, RET_DV), row_map),
                   pl.BlockSpec((1, 1, RET_DK, RET_DV), lambda b, h, c: (b, h, 0, 0))),
        compiler_params=_params("parallel", "parallel", "arbitrary"),
        name="retention_prompt",
    )(q, k, v, g, gn, d_intra, q_dec, k_dec, c_dec)


def _ret_sample_kernel(qt_ref, kt_ref, v_ref, g_ref, gn_ref, dec_ref, s_ref, o_ref, snew_ref):
    for h in range(RET_HEADS):
        qt = qt_ref[0, h]
        kt = kt_ref[0, h]
        v = v_ref[0, h]
        gamma = dec_ref[h]
        s_old = s_ref[0, h]
        qk = jnp.sum(qt * kt, axis=0, keepdims=True)
        o = qk * v + gamma * jnp.sum(qt * s_old, axis=0, keepdims=True)
        snew_ref[0, h] = s_old * gamma + kt * v
        on = o * lax.rsqrt(jnp.mean(o * o, axis=-1, keepdims=True) + EPS)
        g = g_ref[0, h]
        o_ref[0, h] = on * gn_ref[h] * (g * _sigmoid(g))


def _ret_sample(qt, kt, v, g, gn, state):
    nb = state.shape[0]
    lg = jnp.log(1.0 - 2.0 ** (-5.0 - jnp.arange(RET_HEADS, dtype=F32)))
    dec = jnp.broadcast_to(jnp.exp(lg)[:, None, None], (RET_HEADS, 1, RET_DV))
    col = pl.BlockSpec((1, RET_HEADS, RET_DK, 1), lambda b: (b, 0, 0, 0))
    row = pl.BlockSpec((1, RET_HEADS, 1, RET_DV), lambda b: (b, 0, 0, 0))
    per_head = pl.BlockSpec((RET_HEADS, 1, RET_DV), lambda b: (0, 0, 0))
    st = pl.BlockSpec((1, RET_HEADS, RET_DK, RET_DV), lambda b: (b, 0, 0, 0))
    return pl.pallas_call(
        _ret_sample_kernel,
        out_shape=(jax.ShapeDtypeStruct((nb, RET_HEADS, 1, RET_DV), F32),
                   jax.ShapeDtypeStruct(state.shape, F32)),
        grid=(nb,),
        in_specs=[col, col, row, row, per_head, per_head, st],
        out_specs=(row, st),
        compiler_params=_params("parallel"),
        name="retention_sample",
    )(qt, kt, v, g, gn, dec, state)


def _att_prompt_kernel(q_ref, kp_ref, kc_ref, vp_ref, vc_ref, o_ref, lse_ref):
    m = ATT_STREAM_WIN
    ii = lax.broadcasted_iota(jnp.int32, (m, m), 0)
    jj = lax.broadcasted_iota(jnp.int32, (m, m), 1)
    ok_prev = jnp.logical_and(jj >= ii, pl.program_id(2) > 0)
    ok_cur = jj <= ii
    nt = (((1,), (1,)), ((), ()))
    for h in range(ATT_HPG):
        cols = slice(h * ATT_HD, (h + 1) * ATT_HD)
        q = q_ref[0, :, cols]
        s_p = lax.dot_general(q, kp_ref[0, :, cols].astype(BF16), nt, preferred_element_type=F32)
        s_c = lax.dot_general(q, kc_ref[0, :, cols].astype(BF16), nt, preferred_element_type=F32)
        s_p = jnp.where(ok_prev, s_p * ATT_SCALE, MASK_VALUE)
        s_c = jnp.where(ok_cur, s_c * ATT_SCALE, MASK_VALUE)
        mx = jnp.maximum(jnp.max(s_p, axis=-1, keepdims=True), jnp.max(s_c, axis=-1, keepdims=True))
        p_p = jnp.exp(s_p - mx)
        p_c = jnp.exp(s_c - mx)
        l = jnp.sum(p_p, axis=-1, keepdims=True) + jnp.sum(p_c, axis=-1, keepdims=True)
        o = jnp.dot(p_p.astype(BF16), vp_ref[0, :, cols].astype(BF16), preferred_element_type=F32)
        o = o + jnp.dot(p_c.astype(BF16), vc_ref[0, :, cols].astype(BF16), preferred_element_type=F32)
        o_ref[0, :, cols] = o / l
        lse_ref[0, :, cols] = jnp.broadcast_to(mx + jnp.log(l), (m, ATT_HD))


def _att_prompt(q, k, v, group, batch, seq):
    dil = ATT_GROUPS[group][1]
    m = ATT_STREAM_WIN
    length = seq // dil
    nb = length // m
    ngrp = N_GROUPS
    q3 = q.reshape(batch, length, dil * ATT_W)
    k3 = k.reshape(batch, length, dil * ATT_W)
    v3 = v.reshape(batch, length, dil * ATT_W)
    blk = (1, m, ATT_OUT)
    cur = pl.BlockSpec(blk, lambda b, r, n: (b, n, r * ngrp + group))
    prev = pl.BlockSpec(blk, lambda b, r, n: (b, jnp.maximum(n - 1, 0), r * ngrp + group))
    out = pl.BlockSpec(blk, lambda b, r, n: (b, n, r))
    o, lse = pl.pallas_call(
        _att_prompt_kernel,
        out_shape=(jax.ShapeDtypeStruct((batch, length, dil * ATT_OUT), F32),) * 2,
        grid=(batch, dil, nb),
        in_specs=[cur, prev, cur, prev, cur],
        out_specs=(out, out),
        compiler_params=_params("parallel", "parallel", "arbitrary"),
        name="dilated_attention_prompt",
    )(q3, k3, k3, v3, v3)
    return o.reshape(batch * seq, ATT_OUT), lse.reshape(batch * seq, ATT_OUT)


def _att_sample_kernel(q_ref, k_ref, v_ref, c0_ref, c1_ref, c2_ref, *out_refs):
    nt = (((1,), (1,)), ((), ()))
    for g, c_ref in enumerate((c0_ref, c1_ref, c2_ref)):
        o_ref, lse_ref = out_refs[2 * g], out_refs[2 * g + 1]
        for h in range(ATT_HPG):
            cols = slice((g * ATT_HPG + h) * ATT_HD, (g * ATT_HPG + h + 1) * ATT_HD)
            q = q_ref[0, :, cols]
            k_new = k_ref[0, :, cols].astype(BF16).astype(F32)
            v_new = v_ref[0, :, cols].astype(BF16).astype(F32)
            k_win = c_ref[0, :, h * ATT_HD:(h + 1) * ATT_HD].astype(BF16)
            v_win = c_ref[0, :, ATT_OUT + h * ATT_HD:ATT_OUT + (h + 1) * ATT_HD].astype(BF16)
            q8 = jnp.broadcast_to(q, (8, ATT_HD))
            s = lax.dot_general(q8, k_win, nt, preferred_element_type=F32) * ATT_SCALE
            s_new = jnp.sum(q.astype(F32) * k_new, axis=-1, keepdims=True) * ATT_SCALE
            mx = jnp.maximum(jnp.max(s, axis=-1, keepdims=True), s_new)
            p = jnp.exp(s - mx)
            p_new = jnp.exp(s_new - mx)
            l = jnp.sum(p, axis=-1, keepdims=True) + p_new
            o = jnp.dot(p.astype(BF16), v_win, preferred_element_type=F32)
            o = (o + p_new.astype(BF16).astype(F32) * v_new) / l
            hc = slice(h * ATT_HD, (h + 1) * ATT_HD)
            o_ref[0, :, hc] = o[0:1]
            lse_ref[0, :, hc] = jnp.broadcast_to((mx + jnp.log(l))[0:1], (1, ATT_HD))


def _att_sample(q, k, v, caches):
    nb = q.shape[0]
    m = ATT_STREAM_WIN
    row = pl.BlockSpec((1, 1, ATT_W), lambda b: (b, 0, 0))
    win = pl.BlockSpec((1, m, 2 * ATT_OUT), lambda b: (b, 0, 0))
    out = pl.BlockSpec((1, 1, ATT_OUT), lambda b: (b, 0, 0))
    views = [c.reshape(nb, m, dil * 2 * ATT_OUT) for c, (_, dil) in zip(caches, ATT_GROUPS)]
    outs = pl.pallas_call(
        _att_sample_kernel,
        out_shape=(jax.ShapeDtypeStruct((nb, 1, ATT_OUT), F32),) * (2 * N_GROUPS),
        grid=(nb,),
        in_specs=[row, row, row, win, win, win],
        out_specs=(out,) * (2 * N_GROUPS),
        compiler_params=_params("parallel"),
        name="dilated_attention_sample",
    )(q.reshape(nb, 1, ATT_W), k.reshape(nb, 1, ATT_W), v.reshape(nb, 1, ATT_W), *views)
    return [o.reshape(nb, ATT_OUT) for o in outs]


def _merge_kernel(x_ref, xn_ref, or_ref, o0_ref, l0_ref, o1_ref, l1_ref, o2_ref, l2_ref,
                  wg_ref, wpa_ref, wpb_ref, wo_ref, h_ref):
    l0, l1, l2 = l0_ref[...], l1_ref[...], l2_ref[...]
    mx = jnp.maximum(jnp.maximum(l0, l1), l2)
    e0, e1, e2 = jnp.exp(l0 - mx), jnp.exp(l1 - mx), jnp.exp(l2 - mx)
    den = e0 + e1 + e2
    o_a = (e0 / den) * o0_ref[...] + (e1 / den) * o1_ref[...] + (e2 / den) * o2_ref[...]
    branch_b = jnp.dot(o_a.astype(BF16), wpb_ref[...], preferred_element_type=F32)
    branch_a = jnp.dot(or_ref[...], wpa_ref[...], preferred_element_type=F32)
    xn = xn_ref[...]
    gate_a = jnp.dot(xn, wg_ref[:, :D_MODEL], preferred_element_type=F32)
    gate_b = jnp.dot(xn, wg_ref[:, D_MODEL:], preferred_element_type=F32)
    merged = _sigmoid(gate_a) * branch_a + _sigmoid(gate_b) * branch_b
    h_ref[...] = x_ref[...] + jnp.dot(merged.astype(BF16), wo_ref[...], preferred_element_type=F32)


def _merge(x, xn, o_r, att, w_gate, w_pa, w_pb, w_o, tm):
    m = x.shape[0]
    tok = lambda w: pl.BlockSpec((tm, w), lambda i: (i, 0))
    return pl.pallas_call(
        _merge_kernel,
        out_shape=jax.ShapeDtypeStruct((m, D_MODEL), F32),
        grid=(m // tm,),
        in_specs=[tok(D_MODEL), tok(D_MODEL), tok(RET_V)] + [tok(ATT_OUT)] * (2 * N_GROUPS)
                 + [_resident(w_gate.shape), _resident(w_pa.shape), _resident(w_pb.shape),
                    _resident(w_o.shape)],
        out_specs=tok(D_MODEL),
        compiler_params=_params("parallel"),
        name="gated_merge",
    )(x, xn, o_r, *att, w_gate, w_pa, w_pb, w_o)


def _ffn_kernel(h_ref, g2_ref, win_ref, wout_ref, gf_ref, y_ref, t_ref, *, tc):
    h = h_ref[...]
    hn = _rms(h, g2_ref[...]).astype(BF16)
    for c in range(FFN_HIDDEN // tc):
        a = jnp.dot(hn, win_ref[:, c * tc:(c + 1) * tc], preferred_element_type=F32)
        b = jnp.dot(hn, win_ref[:, FFN_HIDDEN + c * tc:FFN_HIDDEN + (c + 1) * tc],
                    preferred_element_type=F32)
        t_ref[:, c * tc:(c + 1) * tc] = (a * _sigmoid(a) * b).astype(BF16)
    h2 = h + jnp.dot(t_ref[...], wout_ref[...], preferred_element_type=F32)
    y_ref[...] = _rms(h2, gf_ref[...])


def _ffn(h, g2, w_in, w_out, gf, tm):
    m = h.shape[0]
    tok = pl.BlockSpec((tm, D_MODEL), lambda i: (i, 0))
    vec = pl.BlockSpec((1, D_MODEL), lambda i: (0, 0))
    return pl.pallas_call(
        functools.partial(_ffn_kernel, tc=256),
        out_shape=jax.ShapeDtypeStruct((m, D_MODEL), F32),
        grid=(m // tm,),
        in_specs=[tok, vec, _resident(w_in.shape), _resident(w_out.shape), vec],
        out_specs=tok,
        scratch_shapes=[pltpu.VMEM((tm, FFN_HIDDEN), BF16)],
        compiler_params=_params("parallel"),
        name="swiglu_final_norm",
    )(h, g2, w_in, w_out, gf)


def _rotary_tables(pos):
    half = RET_DK // 2
    inv = ROPE_BASE ** (-jnp.arange(half, dtype=F32) / half)
    ang = pos.astype(F32)[:, None] * inv[None, :]
    return jnp.cos(ang), jnp.sin(ang)


def _kv_rows(k, v, group, lead):
    cols = slice(group * ATT_OUT, (group + 1) * ATT_OUT)
    kv = jnp.stack([k[:, cols], v[:, cols]], axis=1)
    return kv.reshape(*lead, 2, ATT_HPG, ATT_HD)


def _trunk(x, w, tm_merge, tm_ffn, mixer):
    xn = _rmsnorm(x, w["ln1_g"], tm_ffn)
    o_r, att, extras = mixer(xn)
    h = _merge(x, xn, o_r, att, w["w_gate"], w["w_pa"], w["w_pb"], w["w_o"], tm_merge)
    y = _ffn(h, w["ln2_g"], w["w_ffn_in"], w["w_ffn_out"], w["lnf_g"], tm_ffn)
    return y, extras


def kernel(x_prompt, x_sample, state_ret, cache_kv_w128, cache_kv_w512, cache_kv_w2048,
           ln1_g, w_in, ret_gn_g, w_pa, w_pb, w_o, ln2_g, w_ffn_in, w_ffn_out, lnf_g):
    batch, seq, _ = x_prompt.shape
    nsample = x_sample.shape[0]
    w_in_b = w_in[0].astype(BF16)
    w = {
        "ln1_g": ln1_g, "ln2_g": ln2_g, "lnf_g": lnf_g.reshape(1, D_MODEL),
        "w_gate": w_in_b[:, OFF_GATE:],
        "w_pa": w_pa[0].astype(BF16), "w_pb": w_pb[0].astype(BF16), "w_o": w_o[0].astype(BF16),
        "w_ffn_in": w_ffn_in[0].astype(BF16), "w_ffn_out": w_ffn_out[0].astype(BF16),
    }
    gn = ret_gn_g.reshape(1, RET_V)
    inv_sqrt_dk = RET_DK ** -0.5

    def project(xn, pos, tm):
        rot = _rotary_tables(pos)
        if rot[0].shape[0] < tm:
            rot = tuple(jnp.broadcast_to(t, (tm, RET_DK // 2)) for t in rot)
        q_r = _proj(xn, w_in_b, OFF_QR, RET_QK, BF16, tm, rot=rot)
        k_r = _proj(xn, w_in_b, OFF_KR, RET_QK, BF16, tm, rot=rot, scale=inv_sqrt_dk)
        v_r = _proj(xn, w_in_b, OFF_VR, RET_V, BF16, tm)
        g_r = _proj(xn, w_in_b, OFF_GR, RET_V, BF16, tm)
        q_a = _proj(xn, w_in_b, OFF_QA, ATT_W, BF16, tm)
        k_a = _proj(xn, w_in_b, OFF_KA, ATT_W, F32, tm)
        v_a = _proj(xn, w_in_b, OFF_VA, ATT_W, F32, tm)
        return q_r, k_r, v_r, g_r, q_a, k_a, v_a

    def prompt_mixer(xn):
        q_r, k_r, v_r, g_r, q_a, k_a, v_a = project(xn, jnp.arange(seq, dtype=jnp.int32), 1024)
        o_r, s_new = _ret_prompt(q_r, k_r, v_r, g_r, gn, batch, seq)
        att = []
        for gi in range(N_GROUPS):
            att += list(_att_prompt(q_a, k_a, v_a, gi, batch, seq))
        k4 = k_a.reshape(batch, seq, ATT_W)
        v4 = v_a.reshape(batch, seq, ATT_W)
        kvs = []
        for gi, (win, _) in enumerate(ATT_GROUPS):
            n = min(win, seq)
            kvs.append(_kv_rows(k4[:, seq - n:].reshape(batch * n, ATT_W),
                                v4[:, seq - n:].reshape(batch * n, ATT_W), gi, (1, batch, n)))
        return o_r, att, (s_new[None], kvs)

    def sample_mixer(xn):
        pos = PAST_LEN + jnp.arange(1, dtype=jnp.int32)
        q_r, k_r, v_r, g_r, q_a, k_a, v_a = project(xn, pos, nsample)
        col = (nsample, RET_HEADS, RET_DK, 1)
        row = (nsample, RET_HEADS, 1, RET_DV)
        o_r, s_new = _ret_sample(q_r.astype(F32).reshape(col), k_r.astype(F32).reshape(col),
                                 v_r.astype(F32).reshape(row), g_r.astype(F32).reshape(row),
                                 ret_gn_g.reshape(RET_HEADS, 1, RET_DV), state_ret[0])
        att = _att_sample(q_a, k_a, v_a, (cache_kv_w128[0], cache_kv_w512[0], cache_kv_w2048[0]))
        kvs = [_kv_rows(k_a, v_a, gi, (1, nsample, 1)) for gi in range(N_GROUPS)]
        return o_r.reshape(nsample, RET_V).astype(BF16), att, (s_new[None], kvs)

    y_p, (s_p, kv_p) = _trunk(x_prompt.reshape(batch * seq, D_MODEL), w, 256, 512, prompt_mixer)
    y_s, (s_s, kv_s) = _trunk(x_sample.reshape(nsample, D_MODEL), w, nsample, nsample, sample_mixer)
    return (y_p.reshape(batch, seq, D_MODEL), y_s.reshape(nsample, 1, D_MODEL), s_p, s_s,
            kv_p[0], kv_s[0], kv_p[1], kv_s[1], kv_p[2], kv_s[2])
```

```python
import functools

import jax
import jax.numpy as jnp
from jax import lax
from jax.experimental import pallas as pl
from jax.experimental.pallas import tpu as pltpu

F32 = jnp.float32
BF16 = jnp.bfloat16

D_MODEL = 1024
PAST_LEN = 16384
RET_DK = 256
RET_HEADS = D_MODEL // RET_DK
RET_DV = 2 * RET_DK
RET_QK = RET_HEADS * RET_DK
RET_V = RET_HEADS * RET_DV
ROPE_BASE = 10000.0
ATT_GROUPS = ((128, 1), (512, 4), (2048, 16))
N_GROUPS = len(ATT_GROUPS)
ATT_HPG = 4
ATT_HD = 128
ATT_W = N_GROUPS * ATT_HPG * ATT_HD
ATT_OUT = ATT_HPG * ATT_HD
ATT_STREAM_WIN = 128
FFN_HIDDEN = -(-8 * D_MODEL // (3 * 256)) * 256
EPS = 1e-6
ATT_SCALE = ATT_HD ** -0.5
MASK_VALUE = -1e30
LANES = 128

OFF_QR = 0
OFF_KR = OFF_QR + RET_QK
OFF_VR = OFF_KR + RET_QK
OFF_GR = OFF_VR + RET_V
OFF_QA = OFF_GR + RET_V
OFF_KA = OFF_QA + ATT_W
OFF_VA = OFF_KA + ATT_W
OFF_GATE = OFF_VA + ATT_W

VMEM_LIMIT_BYTES = 56 * 1024 * 1024

RET_CHUNK = 256
RET_ROWS_PER_STEP = 1024
PROJ_TM = 512
PROJ_TN = 512
ATT_BLOCKS_PER_STEP = (4, 1, 1)
MERGE_TM = 256
FFN_TM = 512
SAMPLE_ATT_BATCH = 8


def _params(*sem):
    return pltpu.CompilerParams(dimension_semantics=sem, vmem_limit_bytes=VMEM_LIMIT_BYTES)


def _resident(shape):
    zeros = (0,) * len(shape)
    return pl.BlockSpec(shape, lambda *_: zeros, pipeline_mode=pl.Buffered(1))


def _sigmoid(x):
    return 1.0 / (1.0 + jnp.exp(-x))


def _rms(x, g):
    return x * lax.rsqrt(jnp.mean(x * x, axis=-1, keepdims=True) + EPS) * g


def _dot(a, b):
    return jnp.dot(a, b, preferred_element_type=F32)


def _rmsnorm_kernel(x_ref, g_ref, o_ref):
    o_ref[...] = _rms(x_ref[...], g_ref[...]).astype(o_ref.dtype)


def _rmsnorm(x, g, tm):
    m, d = x.shape
    return pl.pallas_call(
        _rmsnorm_kernel,
        out_shape=jax.ShapeDtypeStruct((m, d), BF16),
        grid=(m // tm,),
        in_specs=[pl.BlockSpec((tm, d), lambda i: (i, 0)),
                  pl.BlockSpec((1, d), lambda i: (0, 0))],
        out_specs=pl.BlockSpec((tm, d), lambda i: (i, 0)),
        compiler_params=_params("parallel"),
        name="rmsnorm",
    )(x, g)


def _proj_kernel(a_ref, w_ref, o_ref):
    o_ref[...] = _dot(a_ref[...], w_ref[...]).astype(o_ref.dtype)


def _rotary_store(acc, cos, sin, scale, o_ref, col):
    half = RET_DK // 2
    x1 = acc[:, :half]
    x2 = acc[:, half:]
    o_ref[:, col:col + half] = ((x1 * cos - x2 * sin) * scale).astype(o_ref.dtype)
    o_ref[:, col + half:col + RET_DK] = ((x1 * sin + x2 * cos) * scale).astype(o_ref.dtype)


def _proj_rot_kernel(a_ref, w_ref, cos_ref, sin_ref, o_ref, *, scale):
    acc = _dot(a_ref[...], w_ref[...])
    for h in range(acc.shape[1] // RET_DK):
        _rotary_store(acc[:, h * RET_DK:(h + 1) * RET_DK], cos_ref[...], sin_ref[...], scale,
                      o_ref, h * RET_DK)


def _proj(xn, w_in, col_off, width, out_dtype, tm, rot=None, scale=1.0):
    m, k = xn.shape
    tn = PROJ_TN
    off = col_off // tn
    in_specs = [pl.BlockSpec((tm, k), lambda i, j: (i, 0)),
                pl.BlockSpec((k, tn), lambda i, j: (0, j + off))]
    args = [xn, w_in]
    if rot is None:
        body = _proj_kernel
    else:
        in_specs += [pl.BlockSpec((tm, RET_DK // 2), lambda i, j: (0, 0))] * 2
        args += list(rot)
        body = functools.partial(_proj_rot_kernel, scale=scale)
    return pl.pallas_call(
        body,
        out_shape=jax.ShapeDtypeStruct((m, width), out_dtype),
        grid=(m // tm, width // tn),
        in_specs=in_specs,
        out_specs=pl.BlockSpec((tm, tn), lambda i, j: (i, j)),
        compiler_params=_params("parallel", "arbitrary"),
        name="in_proj",
    )(*args)


def _ret_proj_kernel(x_ref, g_ref, cos_ref, sin_ref, w_ref, q_ref, k_ref, v_ref, gr_ref, xn_ref):
    xn_ref[...] = _rms(x_ref[...], g_ref[...]).astype(BF16)
    cos = cos_ref[...]
    sin = sin_ref[...]
    for h in range(RET_HEADS):
        c = h * RET_DK
        _rotary_store(_dot(xn_ref[...], w_ref[:, OFF_QR + c:OFF_QR + c + RET_DK]), cos, sin, 1.0,
                      q_ref, c)
        _rotary_store(_dot(xn_ref[...], w_ref[:, OFF_KR + c:OFF_KR + c + RET_DK]), cos, sin,
                      RET_DK ** -0.5, k_ref, c)
    for h in range(RET_HEADS):
        c = h * RET_DV
        v_ref[:, c:c + RET_DV] = _dot(xn_ref[...], w_ref[:, OFF_VR + c:OFF_VR + c + RET_DV]).astype(BF16)
        gr_ref[:, c:c + RET_DV] = _dot(xn_ref[...], w_ref[:, OFF_GR + c:OFF_GR + c + RET_DV]).astype(BF16)


def _ret_proj(x, g, cos, sin, w_ret, seq):
    m = x.shape[0]
    tm = PROJ_TM
    npos = seq // tm
    tok = lambda w: pl.BlockSpec((tm, w), lambda i: (i, 0))
    pos = pl.BlockSpec((tm, RET_DK // 2), lambda i: (i % npos, 0))
    return pl.pallas_call(
        _ret_proj_kernel,
        out_shape=(jax.ShapeDtypeStruct((m, RET_QK), BF16), jax.ShapeDtypeStruct((m, RET_QK), BF16),
                   jax.ShapeDtypeStruct((m, RET_V), BF16), jax.ShapeDtypeStruct((m, RET_V), BF16)),
        grid=(m // tm,),
        in_specs=[tok(D_MODEL), pl.BlockSpec((1, D_MODEL), lambda i: (0, 0)), pos, pos,
                  _resident(w_ret.shape)],
        out_specs=(tok(RET_QK), tok(RET_QK), tok(RET_V), tok(RET_V)),
        scratch_shapes=[pltpu.VMEM((tm, D_MODEL), BF16)],
        compiler_params=_params("parallel"),
        name="retention_proj",
    )(x, g, cos, sin, w_ret)


def _att_proj_kernel(x_ref, g_ref, w_ref, *refs):
    out_refs = refs[:3 * N_GROUPS]
    slab_ref = refs[3 * N_GROUPS]
    xp_refs = refs[3 * N_GROUPS + 1:]
    tm = x_ref.shape[0]
    nslab = D_MODEL // LANES
    xn = _rms(x_ref[...], g_ref[...])
    for c in range(nslab):
        slab_ref[c] = xn[:, c * LANES:(c + 1) * LANES]
    for gi, (_, dil) in enumerate(ATT_GROUPS):
        xp_ref = xp_refs[gi]
        nl = tm // dil
        if dil == 1:
            xp_ref[...] = xn.astype(BF16)
        else:
            for r in range(dil):
                for c in range(nslab):
                    xp_ref[r * nl:(r + 1) * nl, c * LANES:(c + 1) * LANES] = (
                        slab_ref[c, pl.ds(r, nl, stride=dil), :].astype(BF16))
        for kind, off in enumerate((OFF_QA, OFF_KA, OFF_VA)):
            o_ref = out_refs[3 * gi + kind]
            col = off - OFF_QA + gi * ATT_OUT
            res = _dot(xp_ref[...], w_ref[:, col:col + ATT_OUT])
            for r in range(dil):
                o_ref[0, r] = res[r * nl:(r + 1) * nl].astype(o_ref.dtype)


def _att_proj(x, g, w_att, batch, seq):
    tm = PROJ_TM
    nt = seq // tm
    out_shape, out_specs = [], []
    for _, dil in ATT_GROUPS:
        spec = pl.BlockSpec((1, dil, tm // dil, ATT_OUT), lambda i: (i // nt, 0, i % nt, 0))
        for dtype in (BF16, F32, F32):
            out_shape.append(jax.ShapeDtypeStruct((batch, dil, seq // dil, ATT_OUT), dtype))
            out_specs.append(spec)
    return pl.pallas_call(
        _att_proj_kernel,
        out_shape=tuple(out_shape),
        grid=(batch * nt,),
        in_specs=[pl.BlockSpec((tm, D_MODEL), lambda i: (i, 0)),
                  pl.BlockSpec((1, D_MODEL), lambda i: (0, 0)),
                  _resident(w_att.shape)],
        out_specs=tuple(out_specs),
        scratch_shapes=[pltpu.VMEM((D_MODEL // LANES, tm, LANES), F32)]
                       + [pltpu.VMEM((tm, D_MODEL), BF16)] * N_GROUPS,
        compiler_params=_params("parallel"),
        name="attention_proj",
    )(x, g, w_att)


def _ret_prompt_kernel(q_ref, k_ref, v_ref, g_ref, gn_ref, dint_ref, qdec_ref, kdec_ref, cdec_ref,
                       o_ref, s_ref, *, chunk, nchunks):
    @pl.when(pl.program_id(2) == 0)
    def _():
        s_ref[...] = jnp.zeros_like(s_ref)

    dint = dint_ref[0]
    qdec = qdec_ref[0]
    kdec = kdec_ref[0]
    cdec = cdec_ref[0]
    gn = gn_ref[...]

    def body(ci, carry):
        rows = pl.ds(pl.multiple_of(ci * chunk, chunk), chunk)
        q = q_ref[rows, :]
        k = k_ref[rows, :]
        v = v_ref[rows, :]
        s_old = s_ref[0, 0]
        sc = lax.dot_general(q, k, (((1,), (1,)), ((), ())), preferred_element_type=F32) * dint
        o = _dot(sc.astype(BF16), v)
        o = o + qdec * _dot(q, s_old.astype(BF16))
        kd_t = (k.astype(F32) * kdec).T.astype(BF16)
        s_ref[0, 0] = s_old * cdec + _dot(kd_t, v)
        on = o * lax.rsqrt(jnp.mean(o * o, axis=-1, keepdims=True) + EPS)
        g = g_ref[rows, :].astype(F32)
        o_ref[rows, :] = (on * gn * (g * _sigmoid(g))).astype(o_ref.dtype)
        return carry

    lax.fori_loop(0, nchunks, body, 0)


def _ret_tables(chunk):
    lg = jnp.log(1.0 - 2.0 ** (-5.0 - jnp.arange(RET_HEADS, dtype=F32)))
    i = jnp.arange(chunk, dtype=F32)
    diff = i[:, None] - i[None, :]
    d_intra = jnp.where(diff >= 0, jnp.exp(lg[:, None, None] * jnp.maximum(diff, 0.0)), 0.0)
    q_dec = jnp.exp(lg[:, None] * (i + 1.0))[:, :, None]
    k_dec = jnp.exp(lg[:, None] * (chunk - 1.0 - i))[:, :, None]
    c_dec = jnp.exp(lg * chunk)[:, None, None]
    return (d_intra,
            jnp.broadcast_to(q_dec, (RET_HEADS, chunk, RET_DV)),
            jnp.broadcast_to(k_dec, (RET_HEADS, chunk, RET_DK)),
            jnp.broadcast_to(c_dec, (RET_HEADS, 1, RET_DV)))


def _ret_prompt(q, k, v, g, gn, batch, seq):
    rows = RET_ROWS_PER_STEP
    chunk = RET_CHUNK
    nt = seq // rows
    d_intra, q_dec, k_dec, c_dec = _ret_tables(chunk)
    row_map = lambda b, h, c: (b * nt + c, h)
    head_map = lambda b, h, c: (h, 0, 0)
    return pl.pallas_call(
        functools.partial(_ret_prompt_kernel, chunk=chunk, nchunks=rows // chunk),
        out_shape=(jax.ShapeDtypeStruct((batch * seq, RET_V), BF16),
                   jax.ShapeDtypeStruct((batch, RET_HEADS, RET_DK, RET_DV), F32)),
        grid=(batch, RET_HEADS, nt),
        in_specs=[pl.BlockSpec((rows, RET_DK), row_map),
                  pl.BlockSpec((rows, RET_DK), row_map),
                  pl.BlockSpec((rows, RET_DV), row_map),
                  pl.BlockSpec((rows, RET_DV), row_map),
                  pl.BlockSpec((1, RET_DV), lambda b, h, c: (0, h)),
                  pl.BlockSpec((1, chunk, chunk), head_map),
                  pl.BlockSpec((1, chunk, RET_DV), head_map),
                  pl.BlockSpec((1, chunk, RET_DK), head_map),
                  pl.BlockSpec((1, 1, RET_DV), head_map)],
        out_specs=(pl.BlockSpec((rows, RET_DV), row_map),
                   pl.BlockSpec((1, 1, RET_DK, RET_DV), lambda b, h, c: (b, h, 0, 0))),
        compiler_params=_params("parallel", "parallel", "arbitrary"),
        name="retention_prompt",
    )(q, k, v, g, gn, d_intra, q_dec, k_dec, c_dec)


def _ret_sample_kernel(qt_ref, kt_ref, v_ref, g_ref, gn_ref, dec_ref, s_ref, o_ref, snew_ref):
    for h in range(RET_HEADS):
        qt = qt_ref[0, h]
        kt = kt_ref[0, h]
        v = v_ref[0, h]
        gamma = dec_ref[h]
        s_old = s_ref[0, h]
        qk = jnp.sum(qt * kt, axis=0, keepdims=True)
        o = qk * v + gamma * jnp.sum(qt * s_old, axis=0, keepdims=True)
        snew_ref[0, h] = s_old * gamma + kt * v
        on = o * lax.rsqrt(jnp.mean(o * o, axis=-1, keepdims=True) + EPS)
        g = g_ref[0, h]
        o_ref[0, h] = on * gn_ref[h] * (g * _sigmoid(g))


def _ret_sample(qt, kt, v, g, gn, state):
    nb = state.shape[0]
    lg = jnp.log(1.0 - 2.0 ** (-5.0 - jnp.arange(RET_HEADS, dtype=F32)))
    dec = jnp.broadcast_to(jnp.exp(lg)[:, None, None], (RET_HEADS, 1, RET_DV))
    col = pl.BlockSpec((1, RET_HEADS, RET_DK, 1), lambda b: (b, 0, 0, 0))
    row = pl.BlockSpec((1, RET_HEADS, 1, RET_DV), lambda b: (b, 0, 0, 0))
    per_head = pl.BlockSpec((RET_HEADS, 1, RET_DV), lambda b: (0, 0, 0))
    st = pl.BlockSpec((1, RET_HEADS, RET_DK, RET_DV), lambda b: (b, 0, 0, 0))
    return pl.pallas_call(
        _ret_sample_kernel,
        out_shape=(jax.ShapeDtypeStruct((nb, RET_HEADS, 1, RET_DV), F32),
                   jax.ShapeDtypeStruct(state.shape, F32)),
        grid=(nb,),
        in_specs=[col, col, row, row, per_head, per_head, st],
        out_specs=(row, st),
        compiler_params=_params("parallel"),
        name="retention_sample",
    )(qt, kt, v, g, gn, dec, state)


def _att_prompt_kernel(q_ref, kh_ref, kc_ref, vh_ref, vc_ref, o_ref, lse_ref, *, dil, nblk):
    m = ATT_STREAM_WIN
    ii = lax.broadcasted_iota(jnp.int32, (m, m), 0)
    jj = lax.broadcasted_iota(jnp.int32, (m, m), 1)
    band_prev = jj >= ii
    band_cur = jj <= ii
    not_first = pl.program_id(1) > 0
    nt = (((1,), (1,)), ((), ()))

    def stream(r, carry):
        for nb in range(nblk):
            rows = slice(nb * m, (nb + 1) * m)
            ok_prev = jnp.logical_and(band_prev, not_first) if nb == 0 else band_prev
            for h in range(ATT_HPG):
                cols = slice(h * ATT_HD, (h + 1) * ATT_HD)
                q = q_ref[0, r, rows, cols]
                if nb == 0:
                    k_prev, v_prev = kh_ref[0, r, :, cols], vh_ref[0, r, :, cols]
                else:
                    prows = slice((nb - 1) * m, nb * m)
                    k_prev, v_prev = kc_ref[0, r, prows, cols], vc_ref[0, r, prows, cols]
                s_p = lax.dot_general(q, k_prev.astype(BF16), nt, preferred_element_type=F32)
                s_c = lax.dot_general(q, kc_ref[0, r, rows, cols].astype(BF16), nt,
                                      preferred_element_type=F32)
                s_p = jnp.where(ok_prev, s_p * ATT_SCALE, MASK_VALUE)
                s_c = jnp.where(band_cur, s_c * ATT_SCALE, MASK_VALUE)
                mx = jnp.maximum(jnp.max(s_p, axis=-1, keepdims=True),
                                 jnp.max(s_c, axis=-1, keepdims=True))
                p_p = jnp.exp(s_p - mx)
                p_c = jnp.exp(s_c - mx)
                l = jnp.sum(p_p, axis=-1, keepdims=True) + jnp.sum(p_c, axis=-1, keepdims=True)
                o = _dot(p_p.astype(BF16), v_prev.astype(BF16))
                o = o + _dot(p_c.astype(BF16), vc_ref[0, r, rows, cols].astype(BF16))
                lse = jnp.broadcast_to(mx + jnp.log(l), (m, ATT_HD))
                if dil == 1:
                    o_ref[0, h, rows, :] = o / l
                    lse_ref[0, h, rows, :] = lse
                else:
                    trows = pl.ds(nb * m * dil + r, m, stride=dil)
                    o_ref[0, h, trows, :] = o / l
                    lse_ref[0, h, trows, :] = lse
        return carry

    if dil == 1:
        stream(0, 0)
    else:
        lax.fori_loop(0, dil, stream, 0)


def _att_prompt(q, k, v, group, batch, seq):
    dil = ATT_GROUPS[group][1]
    nblk = ATT_BLOCKS_PER_STEP[group]
    m = ATT_STREAM_WIN
    rows = nblk * m
    nsteps = seq // dil // rows
    cur = pl.BlockSpec((1, dil, rows, ATT_OUT), lambda b, j: (b, 0, j, 0))
    halo = pl.BlockSpec((1, dil, m, ATT_OUT), lambda b, j: (b, 0, jnp.maximum(j * nblk - 1, 0), 0))
    out = pl.BlockSpec((1, ATT_HPG, rows * dil, ATT_HD), lambda b, j: (b, 0, j, 0))
    return pl.pallas_call(
        functools.partial(_att_prompt_kernel, dil=dil, nblk=nblk),
        out_shape=(jax.ShapeDtypeStruct((batch, ATT_HPG, seq, ATT_HD), F32),) * 2,
        grid=(batch, nsteps),
        in_specs=[cur, halo, cur, halo, cur],
        out_specs=(out, out),
        compiler_params=_params("parallel", "arbitrary"),
        name="dilated_attention_prompt",
    )(q, k, k, v, v)


def _att_sample_kernel(q_ref, kv_ref, c0_ref, c1_ref, c2_ref, *out_refs):
    half = ATT_HPG

    def one(b, carry):
        for g, c_ref in enumerate((c0_ref, c1_ref, c2_ref)):
            o_ref, lse_ref = out_refs[2 * g], out_refs[2 * g + 1]
            q8 = q_ref[b, g]
            new = kv_ref[b, g]
            win = c_ref[b, :, 0]
            s = jnp.sum(win * q8[None], axis=-1, keepdims=True) * ATT_SCALE
            s_new = jnp.sum(new * q8, axis=-1, keepdims=True) * ATT_SCALE
            mx = jnp.maximum(jnp.max(s, axis=0), s_new)
            p = jnp.exp(s - mx[None])
            p_new = jnp.exp(s_new - mx)
            l = jnp.sum(p, axis=0) + p_new
            pv = pltpu.roll(jnp.broadcast_to(p, win.shape), half, 1)
            acc = jnp.sum(pv * win, axis=0)
            pn = pltpu.roll(jnp.broadcast_to(p_new, new.shape), half, 0)
            ln = pltpu.roll(jnp.broadcast_to(l, new.shape), half, 0)
            o = (acc + pn * new) / ln
            o_ref[b] = o[half:]
            lse_ref[b] = jnp.broadcast_to(mx + jnp.log(l), new.shape)[:half]
        return carry

    lax.fori_loop(0, q_ref.shape[0], one, 0)


def _att_sample(q8, kv_new, caches):
    nb = q8.shape[0]
    bb = SAMPLE_ATT_BATCH
    m = ATT_STREAM_WIN
    tile = 2 * ATT_HPG
    small = pl.BlockSpec((bb, N_GROUPS, tile, ATT_HD), lambda i: (i, 0, 0, 0))
    win = pl.BlockSpec((bb, m, 1, tile, ATT_HD), lambda i: (i, 0, 0, 0, 0))
    out = pl.BlockSpec((bb, ATT_HPG, ATT_HD), lambda i: (i, 0, 0))
    views = [c.reshape(nb, m, dil, tile, ATT_HD) for c, (_, dil) in zip(caches, ATT_GROUPS)]
    return pl.pallas_call(
        _att_sample_kernel,
        out_shape=(jax.ShapeDtypeStruct((nb, ATT_HPG, ATT_HD), F32),) * (2 * N_GROUPS),
        grid=(nb // bb,),
        in_specs=[small, small, win, win, win],
        out_specs=(out,) * (2 * N_GROUPS),
        compiler_params=_params("parallel"),
        name="dilated_attention_sample",
    )(q8, kv_new, *views)


def _merge_kernel(x_ref, g_ref, or_ref, o0_ref, l0_ref, o1_ref, l1_ref, o2_ref, l2_ref,
                  wg_ref, wpa_ref, wpb_ref, wo_ref, h_ref, oa_ref):
    for h in range(ATT_HPG):
        l0, l1, l2 = l0_ref[0, h], l1_ref[0, h], l2_ref[0, h]
        mx = jnp.maximum(jnp.maximum(l0, l1), l2)
        e0, e1, e2 = jnp.exp(l0 - mx), jnp.exp(l1 - mx), jnp.exp(l2 - mx)
        den = e0 + e1 + e2
        o_a = (e0 / den) * o0_ref[0, h] + (e1 / den) * o1_ref[0, h] + (e2 / den) * o2_ref[0, h]
        oa_ref[:, h * ATT_HD:(h + 1) * ATT_HD] = o_a.astype(BF16)
    x = x_ref[...]
    xn = _rms(x, g_ref[...]).astype(BF16)
    branch_b = _dot(oa_ref[...], wpb_ref[...])
    branch_a = _dot(or_ref[...], wpa_ref[...])
    gate_a = _dot(xn, wg_ref[:, :D_MODEL])
    gate_b = _dot(xn, wg_ref[:, D_MODEL:])
    merged = _sigmoid(gate_a) * branch_a + _sigmoid(gate_b) * branch_b
    h_ref[...] = x + _dot(merged.astype(BF16), wo_ref[...])


def _merge(x, g, o_r, att, w_gate, w_pa, w_pb, w_o, tm, seq):
    m = x.shape[0]
    nt = seq // tm
    tok = lambda w: pl.BlockSpec((tm, w), lambda i: (i, 0))
    head = pl.BlockSpec((1, ATT_HPG, tm, ATT_HD), lambda i: (i // nt, 0, i % nt, 0))
    return pl.pallas_call(
        _merge_kernel,
        out_shape=jax.ShapeDtypeStruct((m, D_MODEL), F32),
        grid=(m // tm,),
        in_specs=[tok(D_MODEL), pl.BlockSpec((1, D_MODEL), lambda i: (0, 0)), tok(RET_V)]
                 + [head] * (2 * N_GROUPS)
                 + [_resident(w_gate.shape), _resident(w_pa.shape), _resident(w_pb.shape),
                    _resident(w_o.shape)],
        out_specs=tok(D_MODEL),
        scratch_shapes=[pltpu.VMEM((tm, ATT_OUT), BF16)],
        compiler_params=_params("parallel"),
        name="gated_merge",
    )(x, g, o_r, *att, w_gate, w_pa, w_pb, w_o)


def _ffn_kernel(h_ref, g2_ref, win_ref, wout_ref, gf_ref, y_ref, t_ref, *, tc):
    h = h_ref[...]
    hn = _rms(h, g2_ref[...]).astype(BF16)
    for c in range(FFN_HIDDEN // tc):
        a = _dot(hn, win_ref[:, c * tc:(c + 1) * tc])
        b = _dot(hn, win_ref[:, FFN_HIDDEN + c * tc:FFN_HIDDEN + (c + 1) * tc])
        t_ref[:, c * tc:(c + 1) * tc] = (a * _sigmoid(a) * b).astype(BF16)
    h2 = h + _dot(t_ref[...], wout_ref[...])
    y_ref[...] = _rms(h2, gf_ref[...])


def _ffn(h, g2, w_in, w_out, gf, tm):
    m = h.shape[0]
    tok = pl.BlockSpec((tm, D_MODEL), lambda i: (i, 0))
    vec = pl.BlockSpec((1, D_MODEL), lambda i: (0, 0))
    return pl.pallas_call(
        functools.partial(_ffn_kernel, tc=256),
        out_shape=jax.ShapeDtypeStruct((m, D_MODEL), F32),
        grid=(m // tm,),
        in_specs=[tok, vec, _resident(w_in.shape), _resident(w_out.shape), vec],
        out_specs=tok,
        scratch_shapes=[pltpu.VMEM((tm, FFN_HIDDEN), BF16)],
        compiler_params=_params("parallel"),
        name="swiglu_final_norm",
    )(h, g2, w_in, w_out, gf)


def _rotary_tables(pos):
    half = RET_DK // 2
    inv = ROPE_BASE ** (-jnp.arange(half, dtype=F32) / half)
    ang = pos.astype(F32)[:, None] * inv[None, :]
    return jnp.cos(ang), jnp.sin(ang)


def _window_rows(k, v, win):
    b, dil, length, _ = k.shape
    n = win // dil
    kv = jnp.stack([k[:, :, length - n:], v[:, :, length - n:]], axis=3)
    kv = jnp.swapaxes(kv, 1, 2)
    return kv.reshape(1, b, win, 2, ATT_HPG, ATT_HD)


def kernel(x_prompt, x_sample, state_ret, cache_kv_w128, cache_kv_w512, cache_kv_w2048,
           ln1_g, w_in, ret_gn_g, w_pa, w_pb, w_o, ln2_g, w_ffn_in, w_ffn_out, lnf_g):
    batch, seq, _ = x_prompt.shape
    nsample = x_sample.shape[0]
    w_in_b = w_in[0].astype(BF16)
    w_ret = w_in_b[:, :OFF_QA]
    w_att = w_in_b[:, OFF_QA:OFF_GATE]
    w_gate = w_in_b[:, OFF_GATE:]
    w_pa_b, w_pb_b, w_o_b = w_pa[0].astype(BF16), w_pb[0].astype(BF16), w_o[0].astype(BF16)
    w_ffn_in_b, w_ffn_out_b = w_ffn_in[0].astype(BF16), w_ffn_out[0].astype(BF16)
    lnf = lnf_g.reshape(1, D_MODEL)
    gn = ret_gn_g.reshape(1, RET_V)

    def finish(x, o_r, att, tm_merge, tm_ffn, rows_per_seq):
        h = _merge(x, ln1_g, o_r, att, w_gate, w_pa_b, w_pb_b, w_o_b, tm_merge, rows_per_seq)
        return _ffn(h, ln2_g, w_ffn_in_b, w_ffn_out_b, lnf, tm_ffn)

    xp = x_prompt.reshape(batch * seq, D_MODEL)
    cos, sin = _rotary_tables(jnp.arange(seq, dtype=jnp.int32))
    q_r, k_r, v_r, g_r = _ret_proj(xp, ln1_g, cos, sin, w_ret, seq)
    qkv = _att_proj(xp, ln1_g, w_att, batch, seq)
    o_r, s_p = _ret_prompt(q_r, k_r, v_r, g_r, gn, batch, seq)
    att, kv_p = [], []
    for gi, (win, _) in enumerate(ATT_GROUPS):
        q_g, k_g, v_g = qkv[3 * gi:3 * gi + 3]
        att += list(_att_prompt(q_g, k_g, v_g, gi, batch, seq))
        kv_p.append(_window_rows(k_g, v_g, min(win, seq)))
    y_p = finish(xp, o_r, att, MERGE_TM, FFN_TM, seq)

    xs = x_sample.reshape(nsample, D_MODEL)
    xn_s = _rmsnorm(xs, ln1_g, nsample)
    rot = tuple(jnp.broadcast_to(t, (nsample, RET_DK // 2))
                for t in _rotary_tables(PAST_LEN + jnp.arange(1, dtype=jnp.int32)))
    q_rs = _proj(xn_s, w_in_b, OFF_QR, RET_QK, F32, nsample, rot=rot)
    k_rs = _proj(xn_s, w_in_b, OFF_KR, RET_QK, F32, nsample, rot=rot, scale=RET_DK ** -0.5)
    v_rs = _proj(xn_s, w_in_b, OFF_VR, RET_V, F32, nsample)
    g_rs = _proj(xn_s, w_in_b, OFF_GR, RET_V, F32, nsample)
    q_as = _proj(xn_s, w_in_b, OFF_QA, ATT_W, F32, nsample)
    k_as = _proj(xn_s, w_in_b, OFF_KA, ATT_W, F32, nsample)
    v_as = _proj(xn_s, w_in_b, OFF_VA, ATT_W, F32, nsample)
    col = (nsample, RET_HEADS, RET_DK, 1)
    row = (nsample, RET_HEADS, 1, RET_DV)
    o_rs, s_s = _ret_sample(q_rs.reshape(col), k_rs.reshape(col), v_rs.reshape(row), g_rs.reshape(row),
                            ret_gn_g.reshape(RET_HEADS, 1, RET_DV), state_ret[0])
    grp = (nsample, N_GROUPS, ATT_HPG, ATT_HD)
    q4 = q_as.reshape(grp)
    kv_new = jnp.concatenate([k_as.reshape(grp), v_as.reshape(grp)], axis=2)
    q8 = jnp.concatenate([q4, jnp.zeros_like(q4)], axis=2)
    caches = [c[0].reshape(nsample, c.shape[2], 2 * ATT_HPG, ATT_HD)
              for c in (cache_kv_w128, cache_kv_w512, cache_kv_w2048)]
    att_s = [jnp.swapaxes(a, 0, 1)[None] for a in _att_sample(q8, kv_new, caches)]
    y_s = finish(xs, o_rs.reshape(nsample, RET_V).astype(BF16), att_s, nsample, nsample, nsample)
    kv_s = [kv_new[:, gi].reshape(1, nsample, 1, 2, ATT_HPG, ATT_HD) for gi in range(N_GROUPS)]

    return (y_p.reshape(batch, seq, D_MODEL), y_s.reshape(nsample, 1, D_MODEL), s_p[None], s_s[None],
            kv_p[0], kv_s[0], kv_p[1], kv_s[1], kv_p[2], kv_s[2])
```

```python
import functools

import jax
import jax.numpy as jnp
from jax import lax
from jax.experimental import pallas as pl
from jax.experimental.pallas import tpu as pltpu

F32 = jnp.float32
BF16 = jnp.bfloat16

D_MODEL = 1024
PAST_LEN = 16384
RET_DK = 256
RET_HEADS = D_MODEL // RET_DK
RET_DV = 2 * RET_DK
RET_QK = RET_HEADS * RET_DK
RET_V = RET_HEADS * RET_DV
ROPE_BASE = 10000.0
ATT_GROUPS = ((128, 1), (512, 4), (2048, 16))
N_GROUPS = len(ATT_GROUPS)
ATT_HPG = 4
ATT_HD = 128
ATT_W = N_GROUPS * ATT_HPG * ATT_HD
ATT_OUT = ATT_HPG * ATT_HD
ATT_STREAM_WIN = 128
FFN_HIDDEN = -(-8 * D_MODEL // (3 * 256)) * 256
EPS = 1e-6
ATT_SCALE = ATT_HD ** -0.5
MASK_VALUE = -1e30
LANES = 128

OFF_QR = 0
OFF_KR = OFF_QR + RET_QK
OFF_VR = OFF_KR + RET_QK
OFF_GR = OFF_VR + RET_V
OFF_QA = OFF_GR + RET_V
OFF_KA = OFF_QA + ATT_W
OFF_VA = OFF_KA + ATT_W
OFF_GATE = OFF_VA + ATT_W

VMEM_LIMIT_BYTES = 56 * 1024 * 1024

RET_CHUNK = 256
RET_ROWS_PER_STEP = 1024
PROJ_TM = 512
PROJ_TN = 512
ATT_BLOCKS_PER_STEP = (4, 1, 1)
ATT_STREAMS_PER_ITER = (1, 4, 4)
MERGE_TM = 256
FFN_TM = 512
SAMPLE_ATT_BATCH = 8


def _params(*sem):
    return pltpu.CompilerParams(dimension_semantics=sem, vmem_limit_bytes=VMEM_LIMIT_BYTES)


def _resident(shape):
    zeros = (0,) * len(shape)
    return pl.BlockSpec(shape, lambda *_: zeros, pipeline_mode=pl.Buffered(1))


def _sigmoid(x):
    return 1.0 / (1.0 + jnp.exp(-x))


def _rms(x, g):
    return x * lax.rsqrt(jnp.mean(x * x, axis=-1, keepdims=True) + EPS) * g


def _dot(a, b):
    return jnp.dot(a, b, preferred_element_type=F32)


def _rmsnorm_kernel(x_ref, g_ref, o_ref):
    o_ref[...] = _rms(x_ref[...], g_ref[...]).astype(o_ref.dtype)


def _rmsnorm(x, g, tm):
    m, d = x.shape
    return pl.pallas_call(
        _rmsnorm_kernel,
        out_shape=jax.ShapeDtypeStruct((m, d), BF16),
        grid=(m // tm,),
        in_specs=[pl.BlockSpec((tm, d), lambda i: (i, 0)),
                  pl.BlockSpec((1, d), lambda i: (0, 0))],
        out_specs=pl.BlockSpec((tm, d), lambda i: (i, 0)),
        compiler_params=_params("parallel"),
        name="rmsnorm",
    )(x, g)


def _proj_kernel(a_ref, w_ref, o_ref):
    o_ref[...] = _dot(a_ref[...], w_ref[...]).astype(o_ref.dtype)


def _rotary_store(acc, cos, sin, scale, o_ref, col):
    half = RET_DK // 2
    x1 = acc[:, :half]
    x2 = acc[:, half:]
    o_ref[:, col:col + half] = ((x1 * cos - x2 * sin) * scale).astype(o_ref.dtype)
    o_ref[:, col + half:col + RET_DK] = ((x1 * sin + x2 * cos) * scale).astype(o_ref.dtype)


def _proj_rot_kernel(a_ref, w_ref, cos_ref, sin_ref, o_ref, *, scale):
    acc = _dot(a_ref[...], w_ref[...])
    for h in range(acc.shape[1] // RET_DK):
        _rotary_store(acc[:, h * RET_DK:(h + 1) * RET_DK], cos_ref[...], sin_ref[...], scale,
                      o_ref, h * RET_DK)


def _proj(xn, w_in, col_off, width, out_dtype, tm, rot=None, scale=1.0):
    m, k = xn.shape
    tn = PROJ_TN
    off = col_off // tn
    in_specs = [pl.BlockSpec((tm, k), lambda i, j: (i, 0)),
                pl.BlockSpec((k, tn), lambda i, j: (0, j + off))]
    args = [xn, w_in]
    if rot is None:
        body = _proj_kernel
    else:
        in_specs += [pl.BlockSpec((tm, RET_DK // 2), lambda i, j: (0, 0))] * 2
        args += list(rot)
        body = functools.partial(_proj_rot_kernel, scale=scale)
    return pl.pallas_call(
        body,
        out_shape=jax.ShapeDtypeStruct((m, width), out_dtype),
        grid=(m // tm, width // tn),
        in_specs=in_specs,
        out_specs=pl.BlockSpec((tm, tn), lambda i, j: (i, j)),
        compiler_params=_params("parallel", "arbitrary"),
        name="in_proj",
    )(*args)


def _ret_proj_kernel(x_ref, g_ref, cos_ref, sin_ref, w_ref, q_ref, k_ref, v_ref, gr_ref, xn_ref):
    xn_ref[...] = _rms(x_ref[...], g_ref[...]).astype(BF16)
    cos = cos_ref[...]
    sin = sin_ref[...]
    for h in range(RET_HEADS):
        c = h * RET_DK
        _rotary_store(_dot(xn_ref[...], w_ref[:, OFF_QR + c:OFF_QR + c + RET_DK]), cos, sin, 1.0,
                      q_ref, c)
        _rotary_store(_dot(xn_ref[...], w_ref[:, OFF_KR + c:OFF_KR + c + RET_DK]), cos, sin,
                      RET_DK ** -0.5, k_ref, c)
    for h in range(RET_HEADS):
        c = h * RET_DV
        v_ref[:, c:c + RET_DV] = _dot(xn_ref[...], w_ref[:, OFF_VR + c:OFF_VR + c + RET_DV]).astype(BF16)
        gr_ref[:, c:c + RET_DV] = _dot(xn_ref[...], w_ref[:, OFF_GR + c:OFF_GR + c + RET_DV]).astype(BF16)


def _ret_proj(x, g, cos, sin, w_ret, seq):
    m = x.shape[0]
    tm = PROJ_TM
    npos = seq // tm
    tok = lambda w: pl.BlockSpec((tm, w), lambda i: (i, 0))
    pos = pl.BlockSpec((tm, RET_DK // 2), lambda i: (i % npos, 0))
    return pl.pallas_call(
        _ret_proj_kernel,
        out_shape=(jax.ShapeDtypeStruct((m, RET_QK), BF16), jax.ShapeDtypeStruct((m, RET_QK), BF16),
                   jax.ShapeDtypeStruct((m, RET_V), BF16), jax.ShapeDtypeStruct((m, RET_V), BF16)),
        grid=(m // tm,),
        in_specs=[tok(D_MODEL), pl.BlockSpec((1, D_MODEL), lambda i: (0, 0)), pos, pos,
                  _resident(w_ret.shape)],
        out_specs=(tok(RET_QK), tok(RET_QK), tok(RET_V), tok(RET_V)),
        scratch_shapes=[pltpu.VMEM((tm, D_MODEL), BF16)],
        compiler_params=_params("parallel"),
        name="retention_proj",
    )(x, g, cos, sin, w_ret)


def _att_proj_kernel(x_ref, g_ref, w_ref, *refs):
    out_refs = refs[:3 * N_GROUPS]
    slab_ref = refs[3 * N_GROUPS]
    xp_refs = refs[3 * N_GROUPS + 1:]
    tm = x_ref.shape[0]
    nslab = D_MODEL // LANES
    xn = _rms(x_ref[...], g_ref[...])
    for c in range(nslab):
        slab_ref[c] = xn[:, c * LANES:(c + 1) * LANES]
    for gi, (_, dil) in enumerate(ATT_GROUPS):
        xp_ref = xp_refs[gi]
        nl = tm // dil
        if dil == 1:
            xp_ref[...] = xn.astype(BF16)
        else:
            for r in range(dil):
                for c in range(nslab):
                    xp_ref[r * nl:(r + 1) * nl, c * LANES:(c + 1) * LANES] = (
                        slab_ref[c, pl.ds(r, nl, stride=dil), :].astype(BF16))
        for kind, off in enumerate((OFF_QA, OFF_KA, OFF_VA)):
            o_ref = out_refs[3 * gi + kind]
            col = off - OFF_QA + gi * ATT_OUT
            res = _dot(xp_ref[...], w_ref[:, col:col + ATT_OUT])
            for r in range(dil):
                o_ref[0, r] = res[r * nl:(r + 1) * nl].astype(o_ref.dtype)


def _att_proj(x, g, w_att, batch, seq):
    tm = PROJ_TM
    nt = seq // tm
    out_shape, out_specs = [], []
    for _, dil in ATT_GROUPS:
        spec = pl.BlockSpec((1, dil, tm // dil, ATT_OUT), lambda i: (i // nt, 0, i % nt, 0))
        for dtype in (BF16, F32, F32):
            out_shape.append(jax.ShapeDtypeStruct((batch, dil, seq // dil, ATT_OUT), dtype))
            out_specs.append(spec)
    return pl.pallas_call(
        _att_proj_kernel,
        out_shape=tuple(out_shape),
        grid=(batch * nt,),
        in_specs=[pl.BlockSpec((tm, D_MODEL), lambda i: (i, 0)),
                  pl.BlockSpec((1, D_MODEL), lambda i: (0, 0)),
                  _resident(w_att.shape)],
        out_specs=tuple(out_specs),
        scratch_shapes=[pltpu.VMEM((D_MODEL // LANES, tm, LANES), F32)]
                       + [pltpu.VMEM((tm, D_MODEL), BF16)] * N_GROUPS,
        compiler_params=_params("parallel"),
        name="attention_proj",
    )(x, g, w_att)


def _ret_prompt_kernel(q_ref, k_ref, v_ref, g_ref, gn_ref, dint_ref, qdec_ref, kdec_ref, cdec_ref,
                       o_ref, s_ref, *, chunk, nchunks):
    @pl.when(pl.program_id(2) == 0)
    def _():
        s_ref[...] = jnp.zeros_like(s_ref)

    dint = dint_ref[0]
    qdec = qdec_ref[0]
    kdec = kdec_ref[0]
    cdec = cdec_ref[0]
    gn = gn_ref[...]

    def body(ci, carry):
        rows = pl.ds(pl.multiple_of(ci * chunk, chunk), chunk)
        q = q_ref[rows, :]
        k = k_ref[rows, :]
        v = v_ref[rows, :]
        s_old = s_ref[0, 0]
        sc = lax.dot_general(q, k, (((1,), (1,)), ((), ())), preferred_element_type=F32) * dint
        o = _dot(sc.astype(BF16), v)
        o = o + qdec * _dot(q, s_old.astype(BF16))
        kd_t = (k.astype(F32) * kdec).T.astype(BF16)
        s_ref[0, 0] = s_old * cdec + _dot(kd_t, v)
        on = o * lax.rsqrt(jnp.mean(o * o, axis=-1, keepdims=True) + EPS)
        g = g_ref[rows, :].astype(F32)
        o_ref[rows, :] = (on * gn * (g * _sigmoid(g))).astype(o_ref.dtype)
        return carry

    lax.fori_loop(0, nchunks, body, 0)


def _ret_tables(chunk):
    lg = jnp.log(1.0 - 2.0 ** (-5.0 - jnp.arange(RET_HEADS, dtype=F32)))
    i = jnp.arange(chunk, dtype=F32)
    diff = i[:, None] - i[None, :]
    d_intra = jnp.where(diff >= 0, jnp.exp(lg[:, None, None] * jnp.maximum(diff, 0.0)), 0.0)
    q_dec = jnp.exp(lg[:, None] * (i + 1.0))[:, :, None]
    k_dec = jnp.exp(lg[:, None] * (chunk - 1.0 - i))[:, :, None]
    c_dec = jnp.exp(lg * chunk)[:, None, None]
    return (d_intra,
            jnp.broadcast_to(q_dec, (RET_HEADS, chunk, RET_DV)),
            jnp.broadcast_to(k_dec, (RET_HEADS, chunk, RET_DK)),
            jnp.broadcast_to(c_dec, (RET_HEADS, 1, RET_DV)))


def _ret_prompt(q, k, v, g, gn, batch, seq):
    rows = RET_ROWS_PER_STEP
    chunk = RET_CHUNK
    nt = seq // rows
    d_intra, q_dec, k_dec, c_dec = _ret_tables(chunk)
    row_map = lambda b, h, c: (b * nt + c, h)
    head_map = lambda b, h, c: (h, 0, 0)
    return pl.pallas_call(
        functools.partial(_ret_prompt_kernel, chunk=chunk, nchunks=rows // chunk),
        out_shape=(jax.ShapeDtypeStruct((batch * seq, RET_V), BF16),
                   jax.ShapeDtypeStruct((batch, RET_HEADS, RET_DK, RET_DV), F32)),
        grid=(batch, RET_HEADS, nt),
        in_specs=[pl.BlockSpec((rows, RET_DK), row_map),
                  pl.BlockSpec((rows, RET_DK), row_map),
                  pl.BlockSpec((rows, RET_DV), row_map),
                  pl.BlockSpec((rows, RET_DV), row_map),
                  pl.BlockSpec((1, RET_DV), lambda b, h, c: (0, h)),
                  pl.BlockSpec((1, chunk, chunk), head_map),
                  pl.BlockSpec((1, chunk, RET_DV), head_map),
                  pl.BlockSpec((1, chunk, RET_DK), head_map),
                  pl.BlockSpec((1, 1, RET_DV), head_map)],
        out_specs=(pl.BlockSpec((rows, RET_DV), row_map),
                   pl.BlockSpec((1, 1, RET_DK, RET_DV), lambda b, h, c: (b, h, 0, 0))),
        compiler_params=_params("parallel", "parallel", "arbitrary"),
        name="retention_prompt",
    )(q, k, v, g, gn, d_intra, q_dec, k_dec, c_dec)


def _ret_sample_kernel(qt_ref, kt_ref, v_ref, g_ref, gn_ref, dec_ref, s_ref, o_ref, snew_ref):
    for h in range(RET_HEADS):
        qt = qt_ref[0, h]
        kt = kt_ref[0, h]
        v = v_ref[0, h]
        gamma = dec_ref[h]
        s_old = s_ref[0, h]
        qk = jnp.sum(qt * kt, axis=0, keepdims=True)
        o = qk * v + gamma * jnp.sum(qt * s_old, axis=0, keepdims=True)
        snew_ref[0, h] = s_old * gamma + kt * v
        on = o * lax.rsqrt(jnp.mean(o * o, axis=-1, keepdims=True) + EPS)
        g = g_ref[0, h]
        o_ref[0, h] = on * gn_ref[h] * (g * _sigmoid(g))


def _ret_sample(qt, kt, v, g, gn, state):
    nb = state.shape[0]
    lg = jnp.log(1.0 - 2.0 ** (-5.0 - jnp.arange(RET_HEADS, dtype=F32)))
    dec = jnp.broadcast_to(jnp.exp(lg)[:, None, None], (RET_HEADS, 1, RET_DV))
    col = pl.BlockSpec((1, RET_HEADS, RET_DK, 1), lambda b: (b, 0, 0, 0))
    row = pl.BlockSpec((1, RET_HEADS, 1, RET_DV), lambda b: (b, 0, 0, 0))
    per_head = pl.BlockSpec((RET_HEADS, 1, RET_DV), lambda b: (0, 0, 0))
    st = pl.BlockSpec((1, RET_HEADS, RET_DK, RET_DV), lambda b: (b, 0, 0, 0))
    return pl.pallas_call(
        _ret_sample_kernel,
        out_shape=(jax.ShapeDtypeStruct((nb, RET_HEADS, 1, RET_DV), F32),
                   jax.ShapeDtypeStruct(state.shape, F32)),
        grid=(nb,),
        in_specs=[col, col, row, row, per_head, per_head, st],
        out_specs=(row, st),
        compiler_params=_params("parallel"),
        name="retention_sample",
    )(qt, kt, v, g, gn, dec, state)


def _att_prompt_kernel(q_ref, kh_ref, kc_ref, vh_ref, vc_ref, o_ref, lse_ref, *, dil, nblk, nstr):
    m = ATT_STREAM_WIN
    ii = lax.broadcasted_iota(jnp.int32, (m, 2 * m), 0)
    jj = lax.broadcasted_iota(jnp.int32, (m, 2 * m), 1)
    band = jnp.logical_and(jj >= ii, jj <= ii + m)
    bias = jnp.where(band, 0.0, MASK_VALUE)
    bias_first = jnp.where(jnp.logical_and(band, jj >= m), 0.0, MASK_VALUE)
    bias0 = jnp.where(pl.program_id(1) == 0, bias_first, bias)
    nper = ATT_HPG * nstr

    def heads(x):
        return [x[:, :, h * ATT_HD:(h + 1) * ATT_HD] for h in range(ATT_HPG)]

    def group(r0, carry):
        rs = pl.ds(r0, nstr)
        q = q_ref[0, rs]
        k_cur = kc_ref[0, rs].astype(BF16)
        v_cur = vc_ref[0, rs].astype(BF16)
        k_halo = kh_ref[0, rs].astype(BF16)
        v_halo = vh_ref[0, rs].astype(BF16)
        qs, ks, vs = [], [], []
        for nb in range(nblk):
            rows = slice(nb * m, (nb + 1) * m)
            prows = slice((nb - 1) * m, nb * m)
            k_prev = k_halo if nb == 0 else k_cur[:, prows]
            v_prev = v_halo if nb == 0 else v_cur[:, prows]
            qs += heads(q[:, rows])
            ks += heads(jnp.concatenate([k_prev, k_cur[:, rows]], axis=1))
            vs += heads(jnp.concatenate([v_prev, v_cur[:, rows]], axis=1))
        qb = jnp.concatenate(qs, axis=0)
        kb = jnp.concatenate(ks, axis=0)
        vb = jnp.concatenate(vs, axis=0)
        s = jnp.einsum("bqd,bkd->bqk", qb, kb, preferred_element_type=F32) * ATT_SCALE
        if nblk == 1:
            s = s + bias0[None]
        else:
            s = jnp.concatenate([s[:nper] + bias0[None], s[nper:] + bias[None]], axis=0)
        mx = jnp.max(s, axis=-1, keepdims=True)
        p = jnp.exp(s - mx)
        l = jnp.sum(p, axis=-1, keepdims=True)
        o = jnp.einsum("bqk,bkd->bqd", p.astype(BF16), vb, preferred_element_type=F32) / l
        lse = jnp.broadcast_to(mx + jnp.log(l), o.shape)
        for nb in range(nblk):
            for h in range(ATT_HPG):
                for si in range(nstr):
                    b = (nb * ATT_HPG + h) * nstr + si
                    if dil == 1:
                        trows = slice(nb * m, (nb + 1) * m)
                    else:
                        trows = pl.ds(nb * m * dil + r0 + si, m, stride=dil)
                    o_ref[0, h, trows, :] = o[b]
                    lse_ref[0, h, trows, :] = lse[b]
        return carry

    if dil == nstr:
        group(0, 0)
    else:
        lax.fori_loop(0, dil // nstr, lambda i, c: group(i * nstr, c), 0)


def _att_prompt(q, k, v, group, batch, seq):
    dil = ATT_GROUPS[group][1]
    nblk = ATT_BLOCKS_PER_STEP[group]
    m = ATT_STREAM_WIN
    rows = nblk * m
    nsteps = seq // dil // rows
    cur = pl.BlockSpec((1, dil, rows, ATT_OUT), lambda b, j: (b, 0, j, 0))
    halo = pl.BlockSpec((1, dil, m, ATT_OUT), lambda b, j: (b, 0, jnp.maximum(j * nblk - 1, 0), 0))
    out = pl.BlockSpec((1, ATT_HPG, rows * dil, ATT_HD), lambda b, j: (b, 0, j, 0))
    return pl.pallas_call(
        functools.partial(_att_prompt_kernel, dil=dil, nblk=nblk,
                          nstr=ATT_STREAMS_PER_ITER[group]),
        out_shape=(jax.ShapeDtypeStruct((batch, ATT_HPG, seq, ATT_HD), F32),) * 2,
        grid=(batch, nsteps),
        in_specs=[cur, halo, cur, halo, cur],
        out_specs=(out, out),
        compiler_params=_params("parallel", "arbitrary"),
        name="dilated_attention_prompt",
    )(q, k, k, v, v)


def _att_sample_kernel(q_ref, kv_ref, c0_ref, c1_ref, c2_ref, *out_refs):
    half = ATT_HPG

    def one(b, carry):
        for g, c_ref in enumerate((c0_ref, c1_ref, c2_ref)):
            o_ref, lse_ref = out_refs[2 * g], out_refs[2 * g + 1]
            q8 = q_ref[b, g]
            new = kv_ref[b, g]
            win = c_ref[b, :, 0]
            s = jnp.sum(win * q8[None], axis=-1, keepdims=True) * ATT_SCALE
            s_new = jnp.sum(new * q8, axis=-1, keepdims=True) * ATT_SCALE
            mx = jnp.maximum(jnp.max(s, axis=0), s_new)
            p = jnp.exp(s - mx[None])
            p_new = jnp.exp(s_new - mx)
            l = jnp.sum(p, axis=0) + p_new
            pv = pltpu.roll(jnp.broadcast_to(p, win.shape), half, 1)
            acc = jnp.sum(pv * win, axis=0)
            pn = pltpu.roll(jnp.broadcast_to(p_new, new.shape), half, 0)
            ln = pltpu.roll(jnp.broadcast_to(l, new.shape), half, 0)
            o = (acc + pn * new) / ln
            o_ref[b] = o[half:]
            lse_ref[b] = jnp.broadcast_to(mx + jnp.log(l), new.shape)[:half]
        return carry

    lax.fori_loop(0, q_ref.shape[0], one, 0)


def _att_sample(q8, kv_new, caches):
    nb = q8.shape[0]
    bb = SAMPLE_ATT_BATCH
    m = ATT_STREAM_WIN
    tile = 2 * ATT_HPG
    small = pl.BlockSpec((bb, N_GROUPS, tile, ATT_HD), lambda i: (i, 0, 0, 0))
    win = pl.BlockSpec((bb, m, 1, tile, ATT_HD), lambda i: (i, 0, 0, 0, 0))
    out = pl.BlockSpec((bb, ATT_HPG, ATT_HD), lambda i: (i, 0, 0))
    views = [c.reshape(nb, m, dil, tile, ATT_HD) for c, (_, dil) in zip(caches, ATT_GROUPS)]
    return pl.pallas_call(
        _att_sample_kernel,
        out_shape=(jax.ShapeDtypeStruct((nb, ATT_HPG, ATT_HD), F32),) * (2 * N_GROUPS),
        grid=(nb // bb,),
        in_specs=[small, small, win, win, win],
        out_specs=(out,) * (2 * N_GROUPS),
        compiler_params=_params("parallel"),
        name="dilated_attention_sample",
    )(q8, kv_new, *views)


def _merge_kernel(x_ref, g_ref, or_ref, o0_ref, l0_ref, o1_ref, l1_ref, o2_ref, l2_ref,
                  wg_ref, wpa_ref, wpb_ref, wo_ref, h_ref, oa_ref):
    for h in range(ATT_HPG):
        l0, l1, l2 = l0_ref[0, h], l1_ref[0, h], l2_ref[0, h]
        mx = jnp.maximum(jnp.maximum(l0, l1), l2)
        e0, e1, e2 = jnp.exp(l0 - mx), jnp.exp(l1 - mx), jnp.exp(l2 - mx)
        den = e0 + e1 + e2
        o_a = (e0 / den) * o0_ref[0, h] + (e1 / den) * o1_ref[0, h] + (e2 / den) * o2_ref[0, h]
        oa_ref[:, h * ATT_HD:(h + 1) * ATT_HD] = o_a.astype(BF16)
    x = x_ref[...]
    xn = _rms(x, g_ref[...]).astype(BF16)
    branch_b = _dot(oa_ref[...], wpb_ref[...])
    branch_a = _dot(or_ref[...], wpa_ref[...])
    gate_a = _dot(xn, wg_ref[:, :D_MODEL])
    gate_b = _dot(xn, wg_ref[:, D_MODEL:])
    merged = _sigmoid(gate_a) * branch_a + _sigmoid(gate_b) * branch_b
    h_ref[...] = x + _dot(merged.astype(BF16), wo_ref[...])


def _merge(x, g, o_r, att, w_gate, w_pa, w_pb, w_o, tm, seq):
    m = x.shape[0]
    nt = seq // tm
    tok = lambda w: pl.BlockSpec((tm, w), lambda i: (i, 0))
    head = pl.BlockSpec((1, ATT_HPG, tm, ATT_HD), lambda i: (i // nt, 0, i % nt, 0))
    return pl.pallas_call(
        _merge_kernel,
        out_shape=jax.ShapeDtypeStruct((m, D_MODEL), F32),
        grid=(m // tm,),
        in_specs=[tok(D_MODEL), pl.BlockSpec((1, D_MODEL), lambda i: (0, 0)), tok(RET_V)]
                 + [head] * (2 * N_GROUPS)
                 + [_resident(w_gate.shape), _resident(w_pa.shape), _resident(w_pb.shape),
                    _resident(w_o.shape)],
        out_specs=tok(D_MODEL),
        scratch_shapes=[pltpu.VMEM((tm, ATT_OUT), BF16)],
        compiler_params=_params("parallel"),
        name="gated_merge",
    )(x, g, o_r, *att, w_gate, w_pa, w_pb, w_o)


def _ffn_kernel(h_ref, g2_ref, win_ref, wout_ref, gf_ref, y_ref, t_ref, *, tc):
    h = h_ref[...]
    hn = _rms(h, g2_ref[...]).astype(BF16)
    for c in range(FFN_HIDDEN // tc):
        a = _dot(hn, win_ref[:, c * tc:(c + 1) * tc])
        b = _dot(hn, win_ref[:, FFN_HIDDEN + c * tc:FFN_HIDDEN + (c + 1) * tc])
        t_ref[:, c * tc:(c + 1) * tc] = (a * _sigmoid(a) * b).astype(BF16)
    h2 = h + _dot(t_ref[...], wout_ref[...])
    y_ref[...] = _rms(h2, gf_ref[...])


def _ffn(h, g2, w_in, w_out, gf, tm):
    m = h.shape[0]
    tok = pl.BlockSpec((tm, D_MODEL), lambda i: (i, 0))
    vec = pl.BlockSpec((1, D_MODEL), lambda i: (0, 0))
    return pl.pallas_call(
        functools.partial(_ffn_kernel, tc=256),
        out_shape=jax.ShapeDtypeStruct((m, D_MODEL), F32),
        grid=(m // tm,),
        in_specs=[tok, vec, _resident(w_in.shape), _resident(w_out.shape), vec],
        out_specs=tok,
        scratch_shapes=[pltpu.VMEM((tm, FFN_HIDDEN), BF16)],
        compiler_params=_params("parallel"),
        name="swiglu_final_norm",
    )(h, g2, w_in, w_out, gf)


def _rotary_tables(pos):
    half = RET_DK // 2
    inv = ROPE_BASE ** (-jnp.arange(half, dtype=F32) / half)
    ang = pos.astype(F32)[:, None] * inv[None, :]
    return jnp.cos(ang), jnp.sin(ang)


def _window_rows(k, v, win):
    b, dil, length, _ = k.shape
    n = win // dil
    kv = jnp.stack([k[:, :, length - n:], v[:, :, length - n:]], axis=3)
    kv = jnp.swapaxes(kv, 1, 2)
    return kv.reshape(1, b, win, 2, ATT_HPG, ATT_HD)


def kernel(x_prompt, x_sample, state_ret, cache_kv_w128, cache_kv_w512, cache_kv_w2048,
           ln1_g, w_in, ret_gn_g, w_pa, w_pb, w_o, ln2_g, w_ffn_in, w_ffn_out, lnf_g):
    batch, seq, _ = x_prompt.shape
    nsample = x_sample.shape[0]
    w_in_b = w_in[0].astype(BF16)
    w_ret = w_in_b[:, :OFF_QA]
    w_att = w_in_b[:, OFF_QA:OFF_GATE]
    w_gate = w_in_b[:, OFF_GATE:]
    w_pa_b, w_pb_b, w_o_b = w_pa[0].astype(BF16), w_pb[0].astype(BF16), w_o[0].astype(BF16)
    w_ffn_in_b, w_ffn_out_b = w_ffn_in[0].astype(BF16), w_ffn_out[0].astype(BF16)
    lnf = lnf_g.reshape(1, D_MODEL)
    gn = ret_gn_g.reshape(1, RET_V)

    def finish(x, o_r, att, tm_merge, tm_ffn, rows_per_seq):
        h = _merge(x, ln1_g, o_r, att, w_gate, w_pa_b, w_pb_b, w_o_b, tm_merge, rows_per_seq)
        return _ffn(h, ln2_g, w_ffn_in_b, w_ffn_out_b, lnf, tm_ffn)

    xp = x_prompt.reshape(batch * seq, D_MODEL)
    cos, sin = _rotary_tables(jnp.arange(seq, dtype=jnp.int32))
    q_r, k_r, v_r, g_r = _ret_proj(xp, ln1_g, cos, sin, w_ret, seq)
    qkv = _att_proj(xp, ln1_g, w_att, batch, seq)
    o_r, s_p = _ret_prompt(q_r, k_r, v_r, g_r, gn, batch, seq)
    att, kv_p = [], []
    for gi, (win, _) in enumerate(ATT_GROUPS):
        q_g, k_g, v_g = qkv[3 * gi:3 * gi + 3]
        att += list(_att_prompt(q_g, k_g, v_g, gi, batch, seq))
        kv_p.append(_window_rows(k_g, v_g, min(win, seq)))
    y_p = finish(xp, o_r, att, MERGE_TM, FFN_TM, seq)

    xs = x_sample.reshape(nsample, D_MODEL)
    xn_s = _rmsnorm(xs, ln1_g, nsample)
    rot = tuple(jnp.broadcast_to(t, (nsample, RET_DK // 2))
                for t in _rotary_tables(PAST_LEN + jnp.arange(1, dtype=jnp.int32)))
    q_rs = _proj(xn_s, w_in_b, OFF_QR, RET_QK, F32, nsample, rot=rot)
    k_rs = _proj(xn_s, w_in_b, OFF_KR, RET_QK, F32, nsample, rot=rot, scale=RET_DK ** -0.5)
    v_rs = _proj(xn_s, w_in_b, OFF_VR, RET_V, F32, nsample)
    g_rs = _proj(xn_s, w_in_b, OFF_GR, RET_V, F32, nsample)
    q_as = _proj(xn_s, w_in_b, OFF_QA, ATT_W, F32, nsample)
    k_as = _proj(xn_s, w_in_b, OFF_KA, ATT_W, F32, nsample)
    v_as = _proj(xn_s, w_in_b, OFF_VA, ATT_W, F32, nsample)
    col = (nsample, RET_HEADS, RET_DK, 1)
    row = (nsample, RET_HEADS, 1, RET_DV)
    o_rs, s_s = _ret_sample(q_rs.reshape(col), k_rs.reshape(col), v_rs.reshape(row), g_rs.reshape(row),
                            ret_gn_g.reshape(RET_HEADS, 1, RET_DV), state_ret[0])
    grp = (nsample, N_GROUPS, ATT_HPG, ATT_HD)
    q4 = q_as.reshape(grp)
    kv_new = jnp.concatenate([k_as.reshape(grp), v_as.reshape(grp)], axis=2)
    q8 = jnp.concatenate([q4, jnp.zeros_like(q4)], axis=2)
    caches = [c[0].reshape(nsample, c.shape[2], 2 * ATT_HPG, ATT_HD)
              for c in (cache_kv_w128, cache_kv_w512, cache_kv_w2048)]
    att_s = [jnp.swapaxes(a, 0, 1)[None] for a in _att_sample(q8, kv_new, caches)]
    y_s = finish(xs, o_rs.reshape(nsample, RET_V).astype(BF16), att_s, nsample, nsample, nsample)
    kv_s = [kv_new[:, gi].reshape(1, nsample, 1, 2, ATT_HPG, ATT_HD) for gi in range(N_GROUPS)]

    return (y_p.reshape(batch, seq, D_MODEL), y_s.reshape(nsample, 1, D_MODEL), s_p[None], s_s[None],
            kv_p[0], kv_s[0], kv_p[1], kv_s[1], kv_p[2], kv_s[2])
```

```python
import functools

import jax
import jax.numpy as jnp
from jax import lax
from jax.experimental import pallas as pl
from jax.experimental.pallas import tpu as pltpu

F32 = jnp.float32
BF16 = jnp.bfloat16

D_MODEL = 1024
PAST_LEN = 16384
RET_DK = 256
RET_HEADS = D_MODEL // RET_DK
RET_DV = 2 * RET_DK
RET_QK = RET_HEADS * RET_DK
RET_V = RET_HEADS * RET_DV
ROPE_BASE = 10000.0
ATT_GROUPS = ((128, 1), (512, 4), (2048, 16))
N_GROUPS = len(ATT_GROUPS)
ATT_HPG = 4
ATT_HD = 128
ATT_W = N_GROUPS * ATT_HPG * ATT_HD
ATT_OUT = ATT_HPG * ATT_HD
ATT_STREAM_WIN = 128
FFN_HIDDEN = -(-8 * D_MODEL // (3 * 256)) * 256
EPS = 1e-6
ATT_SCALE = ATT_HD ** -0.5
MASK_VALUE = -1e30
LANES = 128

OFF_QR = 0
OFF_KR = OFF_QR + RET_QK
OFF_VR = OFF_KR + RET_QK
OFF_GR = OFF_VR + RET_V
OFF_QA = OFF_GR + RET_V
OFF_KA = OFF_QA + ATT_W
OFF_VA = OFF_KA + ATT_W
OFF_GATE = OFF_VA + ATT_W

VMEM_LIMIT_BYTES = 56 * 1024 * 1024

RET_CHUNK = 256
RET_ROWS_PER_STEP = 512
PROJ_TM = 512
PROJ_TN = 512
ATT_BLOCKS_PER_STEP = (4, 1, 1)
ATT_STREAMS_PER_ITER = (1, 4, 4)
MERGE_TM = 512
MERGE_TC = 256
FFN_TM = 512
SAMPLE_ATT_BATCH = 8


def _params(*sem):
    return pltpu.CompilerParams(dimension_semantics=sem, vmem_limit_bytes=VMEM_LIMIT_BYTES)


def _resident(shape):
    zeros = (0,) * len(shape)
    return pl.BlockSpec(shape, lambda *_: zeros, pipeline_mode=pl.Buffered(1))


def _sigmoid(x):
    return 1.0 / (1.0 + jnp.exp(-x))


def _rms(x, g):
    return x * lax.rsqrt(jnp.mean(x * x, axis=-1, keepdims=True) + EPS) * g


def _dot(a, b):
    return jnp.dot(a, b, preferred_element_type=F32)


def _rmsnorm_kernel(x_ref, g_ref, o_ref):
    o_ref[...] = _rms(x_ref[...], g_ref[...]).astype(o_ref.dtype)


def _rmsnorm(x, g, tm):
    m, d = x.shape
    return pl.pallas_call(
        _rmsnorm_kernel,
        out_shape=jax.ShapeDtypeStruct((m, d), BF16),
        grid=(m // tm,),
        in_specs=[pl.BlockSpec((tm, d), lambda i: (i, 0)),
                  pl.BlockSpec((1, d), lambda i: (0, 0))],
        out_specs=pl.BlockSpec((tm, d), lambda i: (i, 0)),
        compiler_params=_params("parallel"),
        name="rmsnorm",
    )(x, g)


def _proj_kernel(a_ref, w_ref, o_ref):
    o_ref[...] = _dot(a_ref[...], w_ref[...]).astype(o_ref.dtype)


def _rotary_store(acc, cos, sin, scale, o_ref, col):
    half = RET_DK // 2
    x1 = acc[:, :half]
    x2 = acc[:, half:]
    o_ref[:, col:col + half] = ((x1 * cos - x2 * sin) * scale).astype(o_ref.dtype)
    o_ref[:, col + half:col + RET_DK] = ((x1 * sin + x2 * cos) * scale).astype(o_ref.dtype)


def _proj_rot_kernel(a_ref, w_ref, cos_ref, sin_ref, o_ref, *, scale):
    acc = _dot(a_ref[...], w_ref[...])
    for h in range(acc.shape[1] // RET_DK):
        _rotary_store(acc[:, h * RET_DK:(h + 1) * RET_DK], cos_ref[...], sin_ref[...], scale,
                      o_ref, h * RET_DK)


def _proj(xn, w_in, col_off, width, out_dtype, tm, rot=None, scale=1.0):
    m, k = xn.shape
    tn = PROJ_TN
    off = col_off // tn
    in_specs = [pl.BlockSpec((tm, k), lambda i, j: (i, 0)),
                pl.BlockSpec((k, tn), lambda i, j: (0, j + off))]
    args = [xn, w_in]
    if rot is None:
        body = _proj_kernel
    else:
        in_specs += [pl.BlockSpec((tm, RET_DK // 2), lambda i, j: (0, 0))] * 2
        args += list(rot)
        body = functools.partial(_proj_rot_kernel, scale=scale)
    return pl.pallas_call(
        body,
        out_shape=jax.ShapeDtypeStruct((m, width), out_dtype),
        grid=(m // tm, width // tn),
        in_specs=in_specs,
        out_specs=pl.BlockSpec((tm, tn), lambda i, j: (i, j)),
        compiler_params=_params("parallel", "arbitrary"),
        name="in_proj",
    )(*args)


def _ret_proj_kernel(x_ref, g_ref, cos_ref, sin_ref, w_ref, q_ref, k_ref, v_ref, gr_ref, xn_ref):
    xn_ref[...] = _rms(x_ref[...], g_ref[...]).astype(BF16)
    cos = cos_ref[...]
    sin = sin_ref[...]
    for h in range(RET_HEADS):
        c = h * RET_DK
        _rotary_store(_dot(xn_ref[...], w_ref[:, OFF_QR + c:OFF_QR + c + RET_DK]), cos, sin, 1.0,
                      q_ref, c)
        _rotary_store(_dot(xn_ref[...], w_ref[:, OFF_KR + c:OFF_KR + c + RET_DK]), cos, sin,
                      RET_DK ** -0.5, k_ref, c)
    for h in range(RET_HEADS):
        c = h * RET_DV
        v_ref[:, c:c + RET_DV] = _dot(xn_ref[...], w_ref[:, OFF_VR + c:OFF_VR + c + RET_DV]).astype(BF16)
        gr_ref[:, c:c + RET_DV] = _dot(xn_ref[...], w_ref[:, OFF_GR + c:OFF_GR + c + RET_DV]).astype(BF16)


def _ret_proj(x, g, cos, sin, w_ret, seq):
    m = x.shape[0]
    tm = PROJ_TM
    npos = seq // tm
    tok = lambda w: pl.BlockSpec((tm, w), lambda i: (i, 0))
    pos = pl.BlockSpec((tm, RET_DK // 2), lambda i: (i % npos, 0))
    return pl.pallas_call(
        _ret_proj_kernel,
        out_shape=(jax.ShapeDtypeStruct((m, RET_QK), BF16), jax.ShapeDtypeStruct((m, RET_QK), BF16),
                   jax.ShapeDtypeStruct((m, RET_V), BF16), jax.ShapeDtypeStruct((m, RET_V), BF16)),
        grid=(m // tm,),
        in_specs=[tok(D_MODEL), pl.BlockSpec((1, D_MODEL), lambda i: (0, 0)), pos, pos,
                  _resident(w_ret.shape)],
        out_specs=(tok(RET_QK), tok(RET_QK), tok(RET_V), tok(RET_V)),
        scratch_shapes=[pltpu.VMEM((tm, D_MODEL), BF16)],
        compiler_params=_params("parallel"),
        name="retention_proj",
    )(x, g, cos, sin, w_ret)


def _att_proj_kernel(x_ref, g_ref, w_ref, *refs):
    out_refs = refs[:3 * N_GROUPS]
    slab_ref = refs[3 * N_GROUPS]
    xp_refs = refs[3 * N_GROUPS + 1:]
    tm = x_ref.shape[0]
    nslab = D_MODEL // LANES
    xn = _rms(x_ref[...], g_ref[...])
    for c in range(nslab):
        slab_ref[c] = xn[:, c * LANES:(c + 1) * LANES]
    for gi, (_, dil) in enumerate(ATT_GROUPS):
        xp_ref = xp_refs[gi]
        nl = tm // dil
        if dil == 1:
            xp_ref[...] = xn.astype(BF16)
        else:
            for r in range(dil):
                for c in range(nslab):
                    xp_ref[r * nl:(r + 1) * nl, c * LANES:(c + 1) * LANES] = (
                        slab_ref[c, pl.ds(r, nl, stride=dil), :].astype(BF16))
        for kind, off in enumerate((OFF_QA, OFF_KA, OFF_VA)):
            o_ref = out_refs[3 * gi + kind]
            col = off - OFF_QA + gi * ATT_OUT
            res = _dot(xp_ref[...], w_ref[:, col:col + ATT_OUT])
            for r in range(dil):
                o_ref[0, r] = res[r * nl:(r + 1) * nl].astype(o_ref.dtype)


def _att_proj(x, g, w_att, batch, seq):
    tm = PROJ_TM
    nt = seq // tm
    out_shape, out_specs = [], []
    for _, dil in ATT_GROUPS:
        spec = pl.BlockSpec((1, dil, tm // dil, ATT_OUT), lambda i: (i // nt, 0, i % nt, 0))
        for dtype in (BF16, F32, F32):
            out_shape.append(jax.ShapeDtypeStruct((batch, dil, seq // dil, ATT_OUT), dtype))
            out_specs.append(spec)
    return pl.pallas_call(
        _att_proj_kernel,
        out_shape=tuple(out_shape),
        grid=(batch * nt,),
        in_specs=[pl.BlockSpec((tm, D_MODEL), lambda i: (i, 0)),
                  pl.BlockSpec((1, D_MODEL), lambda i: (0, 0)),
                  _resident(w_att.shape)],
        out_specs=tuple(out_specs),
        scratch_shapes=[pltpu.VMEM((D_MODEL // LANES, tm, LANES), F32)]
                       + [pltpu.VMEM((tm, D_MODEL), BF16)] * N_GROUPS,
        compiler_params=_params("parallel"),
        name="attention_proj",
    )(x, g, w_att)


def _ret_prompt_kernel(q_ref, k_ref, v_ref, g_ref, gn_ref, dint_ref, qdec_ref, kdec_ref, cdec_ref,
                       o_ref, s_ref, *, chunk, nchunks):
    @pl.when(pl.program_id(1) == 0)
    def _():
        s_ref[...] = jnp.zeros_like(s_ref)

    def body(ci, carry):
        rows = pl.ds(pl.multiple_of(ci * chunk, chunk), chunk)
        for h in range(RET_HEADS):
            qk_cols = slice(h * RET_DK, (h + 1) * RET_DK)
            v_cols = slice(h * RET_DV, (h + 1) * RET_DV)
            q = q_ref[rows, qk_cols]
            k = k_ref[rows, qk_cols]
            v = v_ref[rows, v_cols]
            s_old = s_ref[0, h]
            sc = lax.dot_general(q, k, (((1,), (1,)), ((), ())), preferred_element_type=F32)
            o = _dot((sc * dint_ref[h]).astype(BF16), v)
            o = o + qdec_ref[h] * _dot(q, s_old.astype(BF16))
            kd_t = (k.astype(F32) * kdec_ref[h]).T.astype(BF16)
            s_ref[0, h] = s_old * cdec_ref[h] + _dot(kd_t, v)
            on = o * lax.rsqrt(jnp.mean(o * o, axis=-1, keepdims=True) + EPS)
            g = g_ref[rows, v_cols].astype(F32)
            o_ref[rows, v_cols] = (on * gn_ref[:, v_cols] * (g * _sigmoid(g))).astype(o_ref.dtype)
        return carry

    lax.fori_loop(0, nchunks, body, 0)


def _ret_tables(chunk):
    lg = jnp.log(1.0 - 2.0 ** (-5.0 - jnp.arange(RET_HEADS, dtype=F32)))
    i = jnp.arange(chunk, dtype=F32)
    diff = i[:, None] - i[None, :]
    d_intra = jnp.where(diff >= 0, jnp.exp(lg[:, None, None] * jnp.maximum(diff, 0.0)), 0.0)
    q_dec = jnp.exp(lg[:, None] * (i + 1.0))[:, :, None]
    k_dec = jnp.exp(lg[:, None] * (chunk - 1.0 - i))[:, :, None]
    c_dec = jnp.exp(lg * chunk)[:, None, None]
    return (d_intra,
            jnp.broadcast_to(q_dec, (RET_HEADS, chunk, RET_DV)),
            jnp.broadcast_to(k_dec, (RET_HEADS, chunk, RET_DK)),
            jnp.broadcast_to(c_dec, (RET_HEADS, 1, RET_DV)))


def _ret_prompt(q, k, v, g, gn, batch, seq):
    rows = RET_ROWS_PER_STEP
    chunk = RET_CHUNK
    nt = seq // rows
    d_intra, q_dec, k_dec, c_dec = _ret_tables(chunk)
    tok = lambda w: pl.BlockSpec((rows, w), lambda b, c: (b * nt + c, 0))
    const = lambda a: pl.BlockSpec(a.shape, lambda b, c: (0,) * a.ndim)
    return pl.pallas_call(
        functools.partial(_ret_prompt_kernel, chunk=chunk, nchunks=rows // chunk),
        out_shape=(jax.ShapeDtypeStruct((batch * seq, RET_V), BF16),
                   jax.ShapeDtypeStruct((batch, RET_HEADS, RET_DK, RET_DV), F32)),
        grid=(batch, nt),
        in_specs=[tok(RET_QK), tok(RET_QK), tok(RET_V), tok(RET_V), const(gn),
                  const(d_intra), const(q_dec), const(k_dec), const(c_dec)],
        out_specs=(tok(RET_V),
                   pl.BlockSpec((1, RET_HEADS, RET_DK, RET_DV), lambda b, c: (b, 0, 0, 0))),
        compiler_params=_params("parallel", "arbitrary"),
        name="retention_prompt",
    )(q, k, v, g, gn, d_intra, q_dec, k_dec, c_dec)


def _ret_sample_kernel(qt_ref, kt_ref, v_ref, g_ref, gn_ref, dec_ref, s_ref, o_ref, snew_ref):
    for h in range(RET_HEADS):
        qt = qt_ref[0, h]
        kt = kt_ref[0, h]
        v = v_ref[0, h]
        gamma = dec_ref[h]
        s_old = s_ref[0, h]
        qk = jnp.sum(qt * kt, axis=0, keepdims=True)
        o = qk * v + gamma * jnp.sum(qt * s_old, axis=0, keepdims=True)
        snew_ref[0, h] = s_old * gamma + kt * v
        on = o * lax.rsqrt(jnp.mean(o * o, axis=-1, keepdims=True) + EPS)
        g = g_ref[0, h]
        o_ref[0, h] = on * gn_ref[h] * (g * _sigmoid(g))


def _ret_sample(qt, kt, v, g, gn, state):
    nb = state.shape[0]
    lg = jnp.log(1.0 - 2.0 ** (-5.0 - jnp.arange(RET_HEADS, dtype=F32)))
    dec = jnp.broadcast_to(jnp.exp(lg)[:, None, None], (RET_HEADS, 1, RET_DV))
    col = pl.BlockSpec((1, RET_HEADS, RET_DK, 1), lambda b: (b, 0, 0, 0))
    row = pl.BlockSpec((1, RET_HEADS, 1, RET_DV), lambda b: (b, 0, 0, 0))
    per_head = pl.BlockSpec((RET_HEADS, 1, RET_DV), lambda b: (0, 0, 0))
    st = pl.BlockSpec((1, RET_HEADS, RET_DK, RET_DV), lambda b: (b, 0, 0, 0))
    return pl.pallas_call(
        _ret_sample_kernel,
        out_shape=(jax.ShapeDtypeStruct((nb, RET_HEADS, 1, RET_DV), F32),
                   jax.ShapeDtypeStruct(state.shape, F32)),
        grid=(nb,),
        in_specs=[col, col, row, row, per_head, per_head, st],
        out_specs=(row, st),
        compiler_params=_params("parallel"),
        name="retention_sample",
    )(qt, kt, v, g, gn, dec, state)


def _att_prompt_kernel(q_ref, kh_ref, kc_ref, vh_ref, vc_ref, o_ref, lse_ref, *, dil, nblk, nstr):
    m = ATT_STREAM_WIN
    ii = lax.broadcasted_iota(jnp.int32, (m, 2 * m), 0)
    jj = lax.broadcasted_iota(jnp.int32, (m, 2 * m), 1)
    band = jnp.logical_and(jj >= ii, jj <= ii + m)
    bias = jnp.where(band, 0.0, MASK_VALUE)
    bias_first = jnp.where(jnp.logical_and(band, jj >= m), 0.0, MASK_VALUE)
    bias0 = jnp.where(pl.program_id(1) == 0, bias_first, bias)
    nper = ATT_HPG * nstr

    def heads(x):
        return [x[:, :, h * ATT_HD:(h + 1) * ATT_HD] for h in range(ATT_HPG)]

    def group(r0, carry):
        rs = pl.ds(r0, nstr)
        q = q_ref[0, rs]
        k_cur = kc_ref[0, rs].astype(BF16)
        v_cur = vc_ref[0, rs].astype(BF16)
        k_halo = kh_ref[0, rs].astype(BF16)
        v_halo = vh_ref[0, rs].astype(BF16)
        qs, ks, vs = [], [], []
        for nb in range(nblk):
            rows = slice(nb * m, (nb + 1) * m)
            prows = slice((nb - 1) * m, nb * m)
            k_prev = k_halo if nb == 0 else k_cur[:, prows]
            v_prev = v_halo if nb == 0 else v_cur[:, prows]
            qs += heads(q[:, rows])
            ks += heads(jnp.concatenate([k_prev, k_cur[:, rows]], axis=1))
            vs += heads(jnp.concatenate([v_prev, v_cur[:, rows]], axis=1))
        qb = jnp.concatenate(qs, axis=0)
        kb = jnp.concatenate(ks, axis=0)
        vb = jnp.concatenate(vs, axis=0)
        s = jnp.einsum("bqd,bkd->bqk", qb, kb, preferred_element_type=F32) * ATT_SCALE
        if nblk == 1:
            s = s + bias0[None]
        else:
            s = jnp.concatenate([s[:nper] + bias0[None], s[nper:] + bias[None]], axis=0)
        mx = jnp.max(s, axis=-1, keepdims=True)
        p = jnp.exp(s - mx)
        l = jnp.sum(p, axis=-1, keepdims=True)
        o = jnp.einsum("bqk,bkd->bqd", p.astype(BF16), vb, preferred_element_type=F32) / l
        lse = jnp.broadcast_to(mx + jnp.log(l), o.shape)
        for nb in range(nblk):
            for h in range(ATT_HPG):
                for si in range(nstr):
                    b = (nb * ATT_HPG + h) * nstr + si
                    if dil == 1:
                        trows = slice(nb * m, (nb + 1) * m)
                    else:
                        trows = pl.ds(nb * m * dil + r0 + si, m, stride=dil)
                    o_ref[0, h, trows, :] = o[b]
                    lse_ref[0, h, trows, :] = lse[b]
        return carry

    if dil == nstr:
        group(0, 0)
    else:
        lax.fori_loop(0, dil // nstr, lambda i, c: group(i * nstr, c), 0)


def _att_prompt(q, k, v, group, batch, seq):
    dil = ATT_GROUPS[group][1]
    nblk = ATT_BLOCKS_PER_STEP[group]
    m = ATT_STREAM_WIN
    rows = nblk * m
    nsteps = seq // dil // rows
    cur = pl.BlockSpec((1, dil, rows, ATT_OUT), lambda b, j: (b, 0, j, 0))
    halo = pl.BlockSpec((1, dil, m, ATT_OUT), lambda b, j: (b, 0, jnp.maximum(j * nblk - 1, 0), 0))
    out = pl.BlockSpec((1, ATT_HPG, rows * dil, ATT_HD), lambda b, j: (b, 0, j, 0))
    return pl.pallas_call(
        functools.partial(_att_prompt_kernel, dil=dil, nblk=nblk,
                          nstr=ATT_STREAMS_PER_ITER[group]),
        out_shape=(jax.ShapeDtypeStruct((batch, ATT_HPG, seq, ATT_HD), F32),) * 2,
        grid=(batch, nsteps),
        in_specs=[cur, halo, cur, halo, cur],
        out_specs=(out, out),
        compiler_params=_params("parallel", "arbitrary"),
        name="dilated_attention_prompt",
    )(q, k, k, v, v)


def _att_sample_kernel(q_ref, kv_ref, c0_ref, c1_ref, c2_ref, *out_refs):
    half = ATT_HPG

    def one(b, carry):
        for g, c_ref in enumerate((c0_ref, c1_ref, c2_ref)):
            o_ref, lse_ref = out_refs[2 * g], out_refs[2 * g + 1]
            q8 = q_ref[b, g]
            new = kv_ref[b, g]
            win = c_ref[b, :, 0]
            s = jnp.sum(win * q8[None], axis=-1, keepdims=True) * ATT_SCALE
            s_new = jnp.sum(new * q8, axis=-1, keepdims=True) * ATT_SCALE
            mx = jnp.maximum(jnp.max(s, axis=0), s_new)
            p = jnp.exp(s - mx[None])
            p_new = jnp.exp(s_new - mx)
            l = jnp.sum(p, axis=0) + p_new
            pv = pltpu.roll(jnp.broadcast_to(p, win.shape), half, 1)
            acc = jnp.sum(pv * win, axis=0)
            pn = pltpu.roll(jnp.broadcast_to(p_new, new.shape), half, 0)
            ln = pltpu.roll(jnp.broadcast_to(l, new.shape), half, 0)
            o = (acc + pn * new) / ln
            o_ref[b] = o[half:]
            lse_ref[b] = jnp.broadcast_to(mx + jnp.log(l), new.shape)[:half]
        return carry

    lax.fori_loop(0, q_ref.shape[0], one, 0)


def _att_sample(q8, kv_new, caches):
    nb = q8.shape[0]
    bb = SAMPLE_ATT_BATCH
    m = ATT_STREAM_WIN
    tile = 2 * ATT_HPG
    small = pl.BlockSpec((bb, N_GROUPS, tile, ATT_HD), lambda i: (i, 0, 0, 0))
    win = pl.BlockSpec((bb, m, 1, tile, ATT_HD), lambda i: (i, 0, 0, 0, 0))
    out = pl.BlockSpec((bb, ATT_HPG, ATT_HD), lambda i: (i, 0, 0))
    views = [c.reshape(nb, m, dil, tile, ATT_HD) for c, (_, dil) in zip(caches, ATT_GROUPS)]
    return pl.pallas_call(
        _att_sample_kernel,
        out_shape=(jax.ShapeDtypeStruct((nb, ATT_HPG, ATT_HD), F32),) * (2 * N_GROUPS),
        grid=(nb // bb,),
        in_specs=[small, small, win, win, win],
        out_specs=(out,) * (2 * N_GROUPS),
        compiler_params=_params("parallel"),
        name="dilated_attention_sample",
    )(q8, kv_new, *views)


def _merge_kernel(x_ref, g_ref, or_ref, o0_ref, l0_ref, o1_ref, l1_ref, o2_ref, l2_ref,
                  wg_ref, wpa_ref, wpb_ref, wo_ref, h_ref, oa_ref, xn_ref, mg_ref, ta_ref, sb_ref):
    def combine(h):
        l0, l1, l2 = l0_ref[0, h], l1_ref[0, h], l2_ref[0, h]
        mx = jnp.maximum(jnp.maximum(l0, l1), l2)
        e0, e1, e2 = jnp.exp(l0 - mx), jnp.exp(l1 - mx), jnp.exp(l2 - mx)
        o_a = (e0 * o0_ref[0, h] + e1 * o1_ref[0, h] + e2 * o2_ref[0, h]) / (e0 + e1 + e2)
        oa_ref[:, h * ATT_HD:(h + 1) * ATT_HD] = o_a.astype(BF16)

    xn_ref[...] = _rms(x_ref[...], g_ref[...]).astype(BF16)
    tc = MERGE_TC
    nchunk = D_MODEL // tc
    for c in range(nchunk):
        cols = slice(c * tc, (c + 1) * tc)
        gate_a = _dot(xn_ref[...], wg_ref[:, cols])
        gate_b = _dot(xn_ref[...], wg_ref[:, D_MODEL + c * tc:D_MODEL + (c + 1) * tc])
        ta_ref[:, cols] = _sigmoid(gate_a) * _dot(or_ref[...], wpa_ref[:, cols])
        sb_ref[:, cols] = _sigmoid(gate_b)
        for h in range(c * ATT_HPG // nchunk, (c + 1) * ATT_HPG // nchunk):
            combine(h)
    for c in range(nchunk):
        cols = slice(c * tc, (c + 1) * tc)
        branch_b = _dot(oa_ref[...], wpb_ref[:, cols])
        mg_ref[:, cols] = (ta_ref[:, cols] + sb_ref[:, cols] * branch_b).astype(BF16)
    for c in range(nchunk):
        cols = slice(c * tc, (c + 1) * tc)
        h_ref[:, cols] = x_ref[:, cols] + _dot(mg_ref[...], wo_ref[:, cols])


def _merge(x, g, o_r, att, w_gate, w_pa, w_pb, w_o, tm, seq):
    m = x.shape[0]
    nt = seq // tm
    tok = lambda w: pl.BlockSpec((tm, w), lambda i: (i, 0))
    head = pl.BlockSpec((1, ATT_HPG, tm, ATT_HD), lambda i: (i // nt, 0, i % nt, 0))
    return pl.pallas_call(
        _merge_kernel,
        out_shape=jax.ShapeDtypeStruct((m, D_MODEL), F32),
        grid=(m // tm,),
        in_specs=[tok(D_MODEL), pl.BlockSpec((1, D_MODEL), lambda i: (0, 0)), tok(RET_V)]
                 + [head] * (2 * N_GROUPS)
                 + [_resident(w_gate.shape), _resident(w_pa.shape), _resident(w_pb.shape),
                    _resident(w_o.shape)],
        out_specs=tok(D_MODEL),
        scratch_shapes=[pltpu.VMEM((tm, ATT_OUT), BF16), pltpu.VMEM((tm, D_MODEL), BF16),
                        pltpu.VMEM((tm, D_MODEL), BF16), pltpu.VMEM((tm, D_MODEL), F32),
                        pltpu.VMEM((tm, D_MODEL), F32)],
        compiler_params=_params("parallel"),
        name="gated_merge",
    )(x, g, o_r, *att, w_gate, w_pa, w_pb, w_o)


def _ffn_kernel(h_ref, g2_ref, win_ref, wout_ref, gf_ref, y_ref, t_ref, *, tc):
    h = h_ref[...]
    hn = _rms(h, g2_ref[...]).astype(BF16)
    for c in range(FFN_HIDDEN // tc):
        a = _dot(hn, win_ref[:, c * tc:(c + 1) * tc])
        b = _dot(hn, win_ref[:, FFN_HIDDEN + c * tc:FFN_HIDDEN + (c + 1) * tc])
        t_ref[:, c * tc:(c + 1) * tc] = (a * _sigmoid(a) * b).astype(BF16)
    h2 = h + _dot(t_ref[...], wout_ref[...])
    y_ref[...] = _rms(h2, gf_ref[...])


def _ffn(h, g2, w_in, w_out, gf, tm):
    m = h.shape[0]
    tok = pl.BlockSpec((tm, D_MODEL), lambda i: (i, 0))
    vec = pl.BlockSpec((1, D_MODEL), lambda i: (0, 0))
    return pl.pallas_call(
        functools.partial(_ffn_kernel, tc=256),
        out_shape=jax.ShapeDtypeStruct((m, D_MODEL), F32),
        grid=(m // tm,),
        in_specs=[tok, vec, _resident(w_in.shape), _resident(w_out.shape), vec],
        out_specs=tok,
        scratch_shapes=[pltpu.VMEM((tm, FFN_HIDDEN), BF16)],
        compiler_params=_params("parallel"),
        name="swiglu_final_norm",
    )(h, g2, w_in, w_out, gf)


def _rotary_tables(pos):
    half = RET_DK // 2
    inv = ROPE_BASE ** (-jnp.arange(half, dtype=F32) / half)
    ang = pos.astype(F32)[:, None] * inv[None, :]
    return jnp.cos(ang), jnp.sin(ang)


def _window_rows(k, v, win):
    b, dil, length, _ = k.shape
    n = win // dil
    kv = jnp.stack([k[:, :, length - n:], v[:, :, length - n:]], axis=3)
    kv = jnp.swapaxes(kv, 1, 2)
    return kv.reshape(1, b, win, 2, ATT_HPG, ATT_HD)


def kernel(x_prompt, x_sample, state_ret, cache_kv_w128, cache_kv_w512, cache_kv_w2048,
           ln1_g, w_in, ret_gn_g, w_pa, w_pb, w_o, ln2_g, w_ffn_in, w_ffn_out, lnf_g):
    batch, seq, _ = x_prompt.shape
    nsample = x_sample.shape[0]
    w_ret = w_in[0, :, :OFF_QA].astype(BF16)
    w_att = w_in[0, :, OFF_QA:OFF_GATE].astype(BF16)
    w_gate = w_in[0, :, OFF_GATE:].astype(BF16)
    w_pa_b, w_pb_b, w_o_b = w_pa[0].astype(BF16), w_pb[0].astype(BF16), w_o[0].astype(BF16)
    w_ffn_in_b, w_ffn_out_b = w_ffn_in[0].astype(BF16), w_ffn_out[0].astype(BF16)
    lnf = lnf_g.reshape(1, D_MODEL)
    gn = ret_gn_g.reshape(1, RET_V)

    def finish(x, o_r, att, tm_merge, tm_ffn, rows_per_seq):
        h = _merge(x, ln1_g, o_r, att, w_gate, w_pa_b, w_pb_b, w_o_b, tm_merge, rows_per_seq)
        return _ffn(h, ln2_g, w_ffn_in_b, w_ffn_out_b, lnf, tm_ffn)

    xp = x_prompt.reshape(batch * seq, D_MODEL)
    cos, sin = _rotary_tables(jnp.arange(seq, dtype=jnp.int32))
    q_r, k_r, v_r, g_r = _ret_proj(xp, ln1_g, cos, sin, w_ret, seq)
    qkv = _att_proj(xp, ln1_g, w_att, batch, seq)
    o_r, s_p = _ret_prompt(q_r, k_r, v_r, g_r, gn, batch, seq)
    att, kv_p = [], []
    for gi, (win, _) in enumerate(ATT_GROUPS):
        q_g, k_g, v_g = qkv[3 * gi:3 * gi + 3]
        att += list(_att_prompt(q_g, k_g, v_g, gi, batch, seq))
        kv_p.append(_window_rows(k_g, v_g, min(win, seq)))
    y_p = finish(xp, o_r, att, MERGE_TM, FFN_TM, seq)

    xs = x_sample.reshape(nsample, D_MODEL)
    xn_s = _rmsnorm(xs, ln1_g, nsample)
    rot = tuple(jnp.broadcast_to(t, (nsample, RET_DK // 2))
                for t in _rotary_tables(PAST_LEN + jnp.arange(1, dtype=jnp.int32)))
    q_rs = _proj(xn_s, w_ret, OFF_QR, RET_QK, F32, nsample, rot=rot)
    k_rs = _proj(xn_s, w_ret, OFF_KR, RET_QK, F32, nsample, rot=rot, scale=RET_DK ** -0.5)
    v_rs = _proj(xn_s, w_ret, OFF_VR, RET_V, F32, nsample)
    g_rs = _proj(xn_s, w_ret, OFF_GR, RET_V, F32, nsample)
    q_as = _proj(xn_s, w_att, OFF_QA - OFF_QA, ATT_W, F32, nsample)
    k_as = _proj(xn_s, w_att, OFF_KA - OFF_QA, ATT_W, F32, nsample)
    v_as = _proj(xn_s, w_att, OFF_VA - OFF_QA, ATT_W, F32, nsample)
    col = (nsample, RET_HEADS, RET_DK, 1)
    row = (nsample, RET_HEADS, 1, RET_DV)
    o_rs, s_s = _ret_sample(q_rs.reshape(col), k_rs.reshape(col), v_rs.reshape(row), g_rs.reshape(row),
                            ret_gn_g.reshape(RET_HEADS, 1, RET_DV), state_ret[0])
    grp = (nsample, N_GROUPS, ATT_HPG, ATT_HD)
    q4 = q_as.reshape(grp)
    kv_new = jnp.concatenate([k_as.reshape(grp), v_as.reshape(grp)], axis=2)
    q8 = jnp.concatenate([q4, jnp.zeros_like(q4)], axis=2)
    caches = [c[0].reshape(nsample, c.shape[2], 2 * ATT_HPG, ATT_HD)
              for c in (cache_kv_w128, cache_kv_w512, cache_kv_w2048)]
    att_s = [jnp.swapaxes(a, 0, 1)[None] for a in _att_sample(q8, kv_new, caches)]
    y_s = finish(xs, o_rs.reshape(nsample, RET_V).astype(BF16), att_s, nsample, nsample, nsample)
    kv_s = [kv_new[:, gi].reshape(1, nsample, 1, 2, ATT_HPG, ATT_HD) for gi in range(N_GROUPS)]

    return (y_p.reshape(batch, seq, D_MODEL), y_s.reshape(nsample, 1, D_MODEL), s_p[None], s_s[None],
            kv_p[0], kv_s[0], kv_p[1], kv_s[1], kv_p[2], kv_s[2])
```

```python
import functools

import jax
import jax.numpy as jnp
from jax import lax
from jax.experimental import pallas as pl
from jax.experimental.pallas import tpu as pltpu

F32 = jnp.float32
BF16 = jnp.bfloat16

D_MODEL = 1024
PAST_LEN = 16384
RET_DK = 256
RET_HEADS = D_MODEL // RET_DK
RET_DV = 2 * RET_DK
RET_QK = RET_HEADS * RET_DK
RET_V = RET_HEADS * RET_DV
ROPE_BASE = 10000.0
ATT_GROUPS = ((128, 1), (512, 4), (2048, 16))
N_GROUPS = len(ATT_GROUPS)
ATT_HPG = 4
ATT_HD = 128
ATT_W = N_GROUPS * ATT_HPG * ATT_HD
ATT_OUT = ATT_HPG * ATT_HD
ATT_STREAM_WIN = 128
FFN_HIDDEN = -(-8 * D_MODEL // (3 * 256)) * 256
EPS = 1e-6
ATT_SCALE = ATT_HD ** -0.5
MASK_VALUE = -1e30
LANES = 128

OFF_QR = 0
OFF_KR = OFF_QR + RET_QK
OFF_VR = OFF_KR + RET_QK
OFF_GR = OFF_VR + RET_V
OFF_QA = OFF_GR + RET_V
OFF_KA = OFF_QA + ATT_W
OFF_VA = OFF_KA + ATT_W
OFF_GATE = OFF_VA + ATT_W

VMEM_LIMIT_BYTES = 56 * 1024 * 1024

RET_CHUNK = 256
RET_ROWS_PER_STEP = 512
PROJ_TM = 512
PROJ_TN = 512
ATT_BLOCKS_PER_STEP = (4, 1, 1)
ATT_STREAMS_PER_ITER = (1, 4, 4)
MERGE_TM = 512
MERGE_TC = 256
GATE_TN = 512
FFN_TM = 512
SAMPLE_ATT_BATCH = 8


def _params(*sem):
    return pltpu.CompilerParams(dimension_semantics=sem, vmem_limit_bytes=VMEM_LIMIT_BYTES)


def _resident(shape):
    zeros = (0,) * len(shape)
    return pl.BlockSpec(shape, lambda *_: zeros, pipeline_mode=pl.Buffered(1))


def _sigmoid(x):
    return 1.0 / (1.0 + jnp.exp(-x))


def _rms(x, g):
    return x * lax.rsqrt(jnp.mean(x * x, axis=-1, keepdims=True) + EPS) * g


def _dot(a, b):
    return jnp.dot(a, b, preferred_element_type=F32)


def _rmsnorm_kernel(x_ref, g_ref, o_ref):
    o_ref[...] = _rms(x_ref[...], g_ref[...]).astype(o_ref.dtype)


def _rmsnorm(x, g, tm):
    m, d = x.shape
    return pl.pallas_call(
        _rmsnorm_kernel,
        out_shape=jax.ShapeDtypeStruct((m, d), BF16),
        grid=(m // tm,),
        in_specs=[pl.BlockSpec((tm, d), lambda i: (i, 0)),
                  pl.BlockSpec((1, d), lambda i: (0, 0))],
        out_specs=pl.BlockSpec((tm, d), lambda i: (i, 0)),
        compiler_params=_params("parallel"),
        name="rmsnorm",
    )(x, g)


def _proj_kernel(a_ref, w_ref, o_ref):
    o_ref[...] = _dot(a_ref[...], w_ref[...]).astype(o_ref.dtype)


def _rotary_store(acc, cos, sin, scale, o_ref, col):
    half = RET_DK // 2
    x1 = acc[:, :half]
    x2 = acc[:, half:]
    o_ref[:, col:col + half] = ((x1 * cos - x2 * sin) * scale).astype(o_ref.dtype)
    o_ref[:, col + half:col + RET_DK] = ((x1 * sin + x2 * cos) * scale).astype(o_ref.dtype)


def _proj_rot_kernel(a_ref, w_ref, cos_ref, sin_ref, o_ref, *, scale):
    acc = _dot(a_ref[...], w_ref[...])
    for h in range(acc.shape[1] // RET_DK):
        _rotary_store(acc[:, h * RET_DK:(h + 1) * RET_DK], cos_ref[...], sin_ref[...], scale,
                      o_ref, h * RET_DK)


def _proj(xn, w_in, col_off, width, out_dtype, tm, rot=None, scale=1.0):
    m, k = xn.shape
    tn = PROJ_TN
    off = col_off // tn
    in_specs = [pl.BlockSpec((tm, k), lambda i, j: (i, 0)),
                pl.BlockSpec((k, tn), lambda i, j: (0, j + off))]
    args = [xn, w_in]
    if rot is None:
        body = _proj_kernel
    else:
        in_specs += [pl.BlockSpec((tm, RET_DK // 2), lambda i, j: (0, 0))] * 2
        args += list(rot)
        body = functools.partial(_proj_rot_kernel, scale=scale)
    return pl.pallas_call(
        body,
        out_shape=jax.ShapeDtypeStruct((m, width), out_dtype),
        grid=(m // tm, width // tn),
        in_specs=in_specs,
        out_specs=pl.BlockSpec((tm, tn), lambda i, j: (i, j)),
        compiler_params=_params("parallel", "arbitrary"),
        name="in_proj",
    )(*args)


def _ret_proj_kernel(x_ref, g_ref, cos_ref, sin_ref, w_ref, q_ref, k_ref, v_ref, gr_ref, xn_ref):
    xn_ref[...] = _rms(x_ref[...], g_ref[...]).astype(BF16)
    cos = cos_ref[...]
    sin = sin_ref[...]
    for h in range(RET_HEADS):
        c = h * RET_DK
        _rotary_store(_dot(xn_ref[...], w_ref[:, OFF_QR + c:OFF_QR + c + RET_DK]), cos, sin, 1.0,
                      q_ref, c)
        _rotary_store(_dot(xn_ref[...], w_ref[:, OFF_KR + c:OFF_KR + c + RET_DK]), cos, sin,
                      RET_DK ** -0.5, k_ref, c)
    for h in range(RET_HEADS):
        c = h * RET_DV
        v_ref[:, c:c + RET_DV] = _dot(xn_ref[...], w_ref[:, OFF_VR + c:OFF_VR + c + RET_DV]).astype(BF16)
        gr_ref[:, c:c + RET_DV] = _dot(xn_ref[...], w_ref[:, OFF_GR + c:OFF_GR + c + RET_DV]).astype(BF16)


def _ret_proj(x, g, cos, sin, w_in_b, seq):
    m = x.shape[0]
    tm = PROJ_TM
    npos = seq // tm
    tok = lambda w: pl.BlockSpec((tm, w), lambda i: (i, 0))
    pos = pl.BlockSpec((tm, RET_DK // 2), lambda i: (i % npos, 0))
    return pl.pallas_call(
        _ret_proj_kernel,
        out_shape=(jax.ShapeDtypeStruct((m, RET_QK), BF16), jax.ShapeDtypeStruct((m, RET_QK), BF16),
                   jax.ShapeDtypeStruct((m, RET_V), BF16), jax.ShapeDtypeStruct((m, RET_V), BF16)),
        grid=(m // tm,),
        in_specs=[tok(D_MODEL), pl.BlockSpec((1, D_MODEL), lambda i: (0, 0)), pos, pos,
                  pl.BlockSpec((D_MODEL, OFF_QA), lambda i: (0, 0), pipeline_mode=pl.Buffered(1))],
        out_specs=(tok(RET_QK), tok(RET_QK), tok(RET_V), tok(RET_V)),
        scratch_shapes=[pltpu.VMEM((tm, D_MODEL), BF16)],
        compiler_params=_params("parallel"),
        name="retention_proj",
    )(x, g, cos, sin, w_in_b)


def _att_proj_kernel(x_ref, g_ref, wq_ref, wk_ref, wv_ref, *refs, nt):
    qkv_refs = refs[:3 * N_GROUPS]
    win_refs = refs[3 * N_GROUPS:4 * N_GROUPS]
    slab_ref = refs[4 * N_GROUPS]
    xp_refs = refs[4 * N_GROUPS + 1:]
    tm = x_ref.shape[0]
    nslab = D_MODEL // LANES
    tile = pl.program_id(0) % nt
    xn = _rms(x_ref[...], g_ref[...])
    for c in range(nslab):
        slab_ref[c] = xn[:, c * LANES:(c + 1) * LANES]
    for gi, (win, dil) in enumerate(ATT_GROUPS):
        xp_ref = xp_refs[gi]
        nl = tm // dil
        if dil == 1:
            xp_ref[...] = xn.astype(BF16)
        else:
            for r in range(dil):
                for c in range(nslab):
                    xp_ref[r * nl:(r + 1) * nl, c * LANES:(c + 1) * LANES] = (
                        slab_ref[c, pl.ds(r, nl, stride=dil), :].astype(BF16))
        cols = slice(gi * ATT_OUT, (gi + 1) * ATT_OUT)
        res = [_dot(xp_ref[...], w_ref[:, cols]) for w_ref in (wq_ref, wk_ref, wv_ref)]
        for kind in range(3):
            for r in range(dil):
                qkv_refs[3 * gi + kind][0, r] = res[kind][r * nl:(r + 1) * nl].astype(BF16)

        win_ref = win_refs[gi]
        wtok = min(win, tm)

        @pl.when(tile >= nt - max(win // tm, 1))
        def _():
            for c in range(2 * ATT_HPG):
                src = res[1 + c // ATT_HPG][:, (c % ATT_HPG) * ATT_HD:(c % ATT_HPG + 1) * ATT_HD]
                if dil == 1:
                    win_ref[0, pl.ds(c, wtok, stride=2 * ATT_HPG), :] = src[tm - wtok:]
                else:
                    for r in range(dil):
                        win_ref[0, pl.ds(2 * ATT_HPG * r + c, nl, stride=2 * ATT_HPG * dil), :] = (
                            src[r * nl:(r + 1) * nl])


def _att_proj(x, g, w_in_b, batch, seq):
    tm = PROJ_TM
    nt = seq // tm
    tile_rows = 2 * ATT_HPG
    out_shape, out_specs = [], []
    for _, dil in ATT_GROUPS:
        spec = pl.BlockSpec((1, dil, tm // dil, ATT_OUT), lambda i: (i // nt, 0, i % nt, 0))
        out_shape += [jax.ShapeDtypeStruct((batch, dil, seq // dil, ATT_OUT), BF16)] * 3
        out_specs += [spec] * 3
    for win, _ in ATT_GROUPS:
        win = min(win, seq)
        first = nt - max(win // tm, 1)
        out_shape.append(jax.ShapeDtypeStruct((batch, win * tile_rows, ATT_HD), F32))
        out_specs.append(pl.BlockSpec((1, min(win, tm) * tile_rows, ATT_HD),
                                      lambda i, first=first: (i // nt, jnp.maximum(i % nt - first, 0), 0)))
    wcol = lambda kind: pl.BlockSpec((D_MODEL, ATT_W), lambda i: (0, OFF_QA // ATT_W + kind),
                                     pipeline_mode=pl.Buffered(1))
    return pl.pallas_call(
        functools.partial(_att_proj_kernel, nt=nt),
        out_shape=tuple(out_shape),
        grid=(batch * nt,),
        in_specs=[pl.BlockSpec((tm, D_MODEL), lambda i: (i, 0)),
                  pl.BlockSpec((1, D_MODEL), lambda i: (0, 0)),
                  wcol(0), wcol(1), wcol(2)],
        out_specs=tuple(out_specs),
        scratch_shapes=[pltpu.VMEM((D_MODEL // LANES, tm, LANES), F32)]
                       + [pltpu.VMEM((tm, D_MODEL), BF16)] * N_GROUPS,
        compiler_params=_params("arbitrary"),
        name="attention_proj",
    )(x, g, w_in_b, w_in_b, w_in_b)


def _ret_prompt_kernel(q_ref, k_ref, v_ref, g_ref, gn_ref, dint_ref, qdec_ref, kdec_ref, cdec_ref,
                       o_ref, s_ref, *, chunk, nchunks):
    @pl.when(pl.program_id(1) == 0)
    def _():
        s_ref[...] = jnp.zeros_like(s_ref)

    def body(ci, carry):
        rows = pl.ds(pl.multiple_of(ci * chunk, chunk), chunk)
        for h in range(RET_HEADS):
            qk_cols = slice(h * RET_DK, (h + 1) * RET_DK)
            v_cols = slice(h * RET_DV, (h + 1) * RET_DV)
            q = q_ref[rows, qk_cols]
            k = k_ref[rows, qk_cols]
            v = v_ref[rows, v_cols]
            s_old = s_ref[0, h]
            sc = lax.dot_general(q, k, (((1,), (1,)), ((), ())), preferred_element_type=F32)
            o = _dot((sc * dint_ref[h]).astype(BF16), v)
            o = o + qdec_ref[h] * _dot(q, s_old.astype(BF16))
            kd_t = (k.astype(F32) * kdec_ref[h]).T.astype(BF16)
            s_ref[0, h] = s_old * cdec_ref[h] + _dot(kd_t, v)
            on = o * lax.rsqrt(jnp.mean(o * o, axis=-1, keepdims=True) + EPS)
            g = g_ref[rows, v_cols].astype(F32)
            o_ref[rows, v_cols] = (on * gn_ref[:, v_cols] * (g * _sigmoid(g))).astype(o_ref.dtype)
        return carry

    lax.fori_loop(0, nchunks, body, 0)


def _ret_tables(chunk):
    lg = jnp.log(1.0 - 2.0 ** (-5.0 - jnp.arange(RET_HEADS, dtype=F32)))
    i = jnp.arange(chunk, dtype=F32)
    diff = i[:, None] - i[None, :]
    d_intra = jnp.where(diff >= 0, jnp.exp(lg[:, None, None] * jnp.maximum(diff, 0.0)), 0.0)
    q_dec = jnp.exp(lg[:, None] * (i + 1.0))[:, :, None]
    k_dec = jnp.exp(lg[:, None] * (chunk - 1.0 - i))[:, :, None]
    c_dec = jnp.exp(lg * chunk)[:, None, None]
    return (d_intra,
            jnp.broadcast_to(q_dec, (RET_HEADS, chunk, RET_DV)),
            jnp.broadcast_to(k_dec, (RET_HEADS, chunk, RET_DK)),
            jnp.broadcast_to(c_dec, (RET_HEADS, 1, RET_DV)))


def _ret_prompt(q, k, v, g, gn, batch, seq):
    rows = RET_ROWS_PER_STEP
    chunk = RET_CHUNK
    nt = seq // rows
    d_intra, q_dec, k_dec, c_dec = _ret_tables(chunk)
    tok = lambda w: pl.BlockSpec((rows, w), lambda b, c: (b * nt + c, 0))
    const = lambda a: pl.BlockSpec(a.shape, lambda b, c: (0,) * a.ndim)
    return pl.pallas_call(
        functools.partial(_ret_prompt_kernel, chunk=chunk, nchunks=rows // chunk),
        out_shape=(jax.ShapeDtypeStruct((batch * seq, RET_V), BF16),
                   jax.ShapeDtypeStruct((batch, RET_HEADS, RET_DK, RET_DV), F32)),
        grid=(batch, nt),
        in_specs=[tok(RET_QK), tok(RET_QK), tok(RET_V), tok(RET_V), const(gn),
                  const(d_intra), const(q_dec), const(k_dec), const(c_dec)],
        out_specs=(tok(RET_V),
                   pl.BlockSpec((1, RET_HEADS, RET_DK, RET_DV), lambda b, c: (b, 0, 0, 0))),
        compiler_params=_params("parallel", "arbitrary"),
        name="retention_prompt",
    )(q, k, v, g, gn, d_intra, q_dec, k_dec, c_dec)


def _ret_sample_kernel(qt_ref, kt_ref, v_ref, g_ref, gn_ref, dec_ref, s_ref, o_ref, snew_ref):
    for h in range(RET_HEADS):
        qt = qt_ref[0, h]
        kt = kt_ref[0, h]
        v = v_ref[0, h]
        gamma = dec_ref[h]
        s_old = s_ref[0, h]
        qk = jnp.sum(qt * kt, axis=0, keepdims=True)
        o = qk * v + gamma * jnp.sum(qt * s_old, axis=0, keepdims=True)
        snew_ref[0, h] = s_old * gamma + kt * v
        on = o * lax.rsqrt(jnp.mean(o * o, axis=-1, keepdims=True) + EPS)
        g = g_ref[0, h]
        o_ref[0, h] = on * gn_ref[h] * (g * _sigmoid(g))


def _ret_sample(qt, kt, v, g, gn, state):
    nb = state.shape[0]
    lg = jnp.log(1.0 - 2.0 ** (-5.0 - jnp.arange(RET_HEADS, dtype=F32)))
    dec = jnp.broadcast_to(jnp.exp(lg)[:, None, None], (RET_HEADS, 1, RET_DV))
    col = pl.BlockSpec((1, RET_HEADS, RET_DK, 1), lambda b: (b, 0, 0, 0))
    row = pl.BlockSpec((1, RET_HEADS, 1, RET_DV), lambda b: (b, 0, 0, 0))
    per_head = pl.BlockSpec((RET_HEADS, 1, RET_DV), lambda b: (0, 0, 0))
    st = pl.BlockSpec((1, RET_HEADS, RET_DK, RET_DV), lambda b: (b, 0, 0, 0))
    return pl.pallas_call(
        _ret_sample_kernel,
        out_shape=(jax.ShapeDtypeStruct((nb, RET_HEADS, 1, RET_DV), F32),
                   jax.ShapeDtypeStruct(state.shape, F32)),
        grid=(nb,),
        in_specs=[col, col, row, row, per_head, per_head, st],
        out_specs=(row, st),
        compiler_params=_params("parallel"),
        name="retention_sample",
    )(qt, kt, v, g, gn, dec, state)


def _att_prompt_kernel(q_ref, kh_ref, kc_ref, vh_ref, vc_ref, o_ref, lse_ref, *, dil, nblk, nstr):
    m = ATT_STREAM_WIN
    ii = lax.broadcasted_iota(jnp.int32, (m, 2 * m), 0)
    jj = lax.broadcasted_iota(jnp.int32, (m, 2 * m), 1)
    band = jnp.logical_and(jj >= ii, jj <= ii + m)
    bias = jnp.where(band, 0.0, MASK_VALUE)
    bias_first = jnp.where(jnp.logical_and(band, jj >= m), 0.0, MASK_VALUE)
    bias0 = jnp.where(pl.program_id(1) == 0, bias_first, bias)
    nper = ATT_HPG * nstr

    def heads(x):
        return [x[:, :, h * ATT_HD:(h + 1) * ATT_HD] for h in range(ATT_HPG)]

    def group(r0, carry):
        rs = pl.ds(r0, nstr)
        q = q_ref[0, rs]
        k_cur = kc_ref[0, rs].astype(BF16)
        v_cur = vc_ref[0, rs].astype(BF16)
        k_halo = kh_ref[0, rs].astype(BF16)
        v_halo = vh_ref[0, rs].astype(BF16)
        qs, ks, vs = [], [], []
        for nb in range(nblk):
            rows = slice(nb * m, (nb + 1) * m)
            prows = slice((nb - 1) * m, nb * m)
            k_prev = k_halo if nb == 0 else k_cur[:, prows]
            v_prev = v_halo if nb == 0 else v_cur[:, prows]
            qs += heads(q[:, rows])
            ks += heads(jnp.concatenate([k_prev, k_cur[:, rows]], axis=1))
            vs += heads(jnp.concatenate([v_prev, v_cur[:, rows]], axis=1))
        qb = jnp.concatenate(qs, axis=0)
        kb = jnp.concatenate(ks, axis=0)
        vb = jnp.concatenate(vs, axis=0)
        s = jnp.einsum("bqd,bkd->bqk", qb, kb, preferred_element_type=F32) * ATT_SCALE
        if nblk == 1:
            s = s + bias0[None]
        else:
            s = jnp.concatenate([s[:nper] + bias0[None], s[nper:] + bias[None]], axis=0)
        mx = jnp.max(s, axis=-1, keepdims=True)
        p = jnp.exp(s - mx)
        l = jnp.sum(p, axis=-1, keepdims=True)
        o = jnp.einsum("bqk,bkd->bqd", p.astype(BF16), vb, preferred_element_type=F32) / l
        lse = jnp.broadcast_to(mx + jnp.log(l), o.shape)
        for nb in range(nblk):
            for h in range(ATT_HPG):
                for si in range(nstr):
                    b = (nb * ATT_HPG + h) * nstr + si
                    if dil == 1:
                        trows = slice(nb * m, (nb + 1) * m)
                    else:
                        trows = pl.ds(nb * m * dil + r0 + si, m, stride=dil)
                    o_ref[0, h, trows, :] = o[b]
                    lse_ref[0, h, trows, :] = lse[b]
        return carry

    if dil == nstr:
        group(0, 0)
    else:
        lax.fori_loop(0, dil // nstr, lambda i, c: group(i * nstr, c), 0)


def _att_prompt(q, k, v, group, batch, seq):
    dil = ATT_GROUPS[group][1]
    nblk = ATT_BLOCKS_PER_STEP[group]
    m = ATT_STREAM_WIN
    rows = nblk * m
    nsteps = seq // dil // rows
    cur = pl.BlockSpec((1, dil, rows, ATT_OUT), lambda b, j: (b, 0, j, 0))
    halo = pl.BlockSpec((1, dil, m, ATT_OUT), lambda b, j: (b, 0, jnp.maximum(j * nblk - 1, 0), 0))
    out = pl.BlockSpec((1, ATT_HPG, rows * dil, ATT_HD), lambda b, j: (b, 0, j, 0))
    return pl.pallas_call(
        functools.partial(_att_prompt_kernel, dil=dil, nblk=nblk,
                          nstr=ATT_STREAMS_PER_ITER[group]),
        out_shape=(jax.ShapeDtypeStruct((batch, ATT_HPG, seq, ATT_HD), F32),) * 2,
        grid=(batch, nsteps),
        in_specs=[cur, halo, cur, halo, cur],
        out_specs=(out, out),
        compiler_params=_params("parallel", "arbitrary"),
        name="dilated_attention_prompt",
    )(q, k, k, v, v)


def _att_sample_kernel(q_ref, kv_ref, c0_ref, c1_ref, c2_ref, *out_refs):
    half = ATT_HPG

    def one(b, carry):
        for g, c_ref in enumerate((c0_ref, c1_ref, c2_ref)):
            o_ref, lse_ref = out_refs[2 * g], out_refs[2 * g + 1]
            q8 = q_ref[b, g]
            new = kv_ref[b, g]
            win = c_ref[b, :, 0]
            s = jnp.sum(win * q8[None], axis=-1, keepdims=True) * ATT_SCALE
            s_new = jnp.sum(new * q8, axis=-1, keepdims=True) * ATT_SCALE
            mx = jnp.maximum(jnp.max(s, axis=0), s_new)
            p = jnp.exp(s - mx[None])
            p_new = jnp.exp(s_new - mx)
            l = jnp.sum(p, axis=0) + p_new
            pv = pltpu.roll(jnp.broadcast_to(p, win.shape), half, 1)
            acc = jnp.sum(pv * win, axis=0)
            pn = pltpu.roll(jnp.broadcast_to(p_new, new.shape), half, 0)
            ln = pltpu.roll(jnp.broadcast_to(l, new.shape), half, 0)
            o = (acc + pn * new) / ln
            o_ref[b] = o[half:]
            lse_ref[b] = jnp.broadcast_to(mx + jnp.log(l), new.shape)[:half]
        return carry

    lax.fori_loop(0, q_ref.shape[0], one, 0)


def _att_sample(q8, kv_new, caches):
    nb = q8.shape[0]
    bb = SAMPLE_ATT_BATCH
    m = ATT_STREAM_WIN
    tile = 2 * ATT_HPG
    small = pl.BlockSpec((bb, N_GROUPS, tile, ATT_HD), lambda i: (i, 0, 0, 0))
    win = pl.BlockSpec((bb, m, 1, tile, ATT_HD), lambda i: (i, 0, 0, 0, 0))
    out = pl.BlockSpec((bb, ATT_HPG, ATT_HD), lambda i: (i, 0, 0))
    views = [c.reshape(nb, m, dil, tile, ATT_HD) for c, (_, dil) in zip(caches, ATT_GROUPS)]
    return pl.pallas_call(
        _att_sample_kernel,
        out_shape=(jax.ShapeDtypeStruct((nb, ATT_HPG, ATT_HD), F32),) * (2 * N_GROUPS),
        grid=(nb // bb,),
        in_specs=[small, small, win, win, win],
        out_specs=(out,) * (2 * N_GROUPS),
        compiler_params=_params("parallel"),
        name="dilated_attention_sample",
    )(q8, kv_new, *views)


def _merge_kernel(x_ref, g_ref, or_ref, o0_ref, l0_ref, o1_ref, l1_ref, o2_ref, l2_ref,
                  wga0_ref, wga1_ref, wgb0_ref, wgb1_ref, wpa_ref, wpb_ref, wo_ref,
                  h_ref, oa_ref, xn_ref, mg_ref, ta_ref, sb_ref):
    def combine(h):
        l0, l1, l2 = l0_ref[0, h], l1_ref[0, h], l2_ref[0, h]
        mx = jnp.maximum(jnp.maximum(l0, l1), l2)
        e0, e1, e2 = jnp.exp(l0 - mx), jnp.exp(l1 - mx), jnp.exp(l2 - mx)
        o_a = (e0 * o0_ref[0, h] + e1 * o1_ref[0, h] + e2 * o2_ref[0, h]) / (e0 + e1 + e2)
        oa_ref[:, h * ATT_HD:(h + 1) * ATT_HD] = o_a.astype(BF16)

    xn_ref[...] = _rms(x_ref[...], g_ref[...]).astype(BF16)
    tc = MERGE_TC
    nchunk = D_MODEL // tc
    for c in range(nchunk):
        cols = slice(c * tc, (c + 1) * tc)
        half, hcols = divmod(c * tc, GATE_TN)
        gate_a = _dot(xn_ref[...], (wga0_ref, wga1_ref)[half][:, hcols:hcols + tc])
        gate_b = _dot(xn_ref[...], (wgb0_ref, wgb1_ref)[half][:, hcols:hcols + tc])
        ta_ref[:, cols] = _sigmoid(gate_a) * _dot(or_ref[...], wpa_ref[:, cols])
        sb_ref[:, cols] = _sigmoid(gate_b)
        for h in range(c * ATT_HPG // nchunk, (c + 1) * ATT_HPG // nchunk):
            combine(h)
    for c in range(nchunk):
        cols = slice(c * tc, (c + 1) * tc)
        branch_b = _dot(oa_ref[...], wpb_ref[:, cols])
        mg_ref[:, cols] = (ta_ref[:, cols] + sb_ref[:, cols] * branch_b).astype(BF16)
    for c in range(nchunk):
        cols = slice(c * tc, (c + 1) * tc)
        h_ref[:, cols] = x_ref[:, cols] + _dot(mg_ref[...], wo_ref[:, cols])


def _merge(x, g, o_r, att, w_in_b, w_pa, w_pb, w_o, tm, seq):
    m = x.shape[0]
    nt = seq // tm
    tok = lambda w: pl.BlockSpec((tm, w), lambda i: (i, 0))
    head = pl.BlockSpec((1, ATT_HPG, tm, ATT_HD), lambda i: (i // nt, 0, i % nt, 0))
    return pl.pallas_call(
        _merge_kernel,
        out_shape=jax.ShapeDtypeStruct((m, D_MODEL), F32),
        grid=(m // tm,),
        in_specs=[tok(D_MODEL), pl.BlockSpec((1, D_MODEL), lambda i: (0, 0)), tok(RET_V)]
                 + [head] * (2 * N_GROUPS)
                 + [pl.BlockSpec((D_MODEL, GATE_TN), lambda i, j=j: (0, OFF_GATE // GATE_TN + j),
                                 pipeline_mode=pl.Buffered(1)) for j in range(2 * D_MODEL // GATE_TN)]
                 + [_resident(w_pa.shape), _resident(w_pb.shape),
                    _resident(w_o.shape)],
        out_specs=tok(D_MODEL),
        scratch_shapes=[pltpu.VMEM((tm, ATT_OUT), BF16), pltpu.VMEM((tm, D_MODEL), BF16),
                        pltpu.VMEM((tm, D_MODEL), BF16), pltpu.VMEM((tm, D_MODEL), F32),
                        pltpu.VMEM((tm, D_MODEL), F32)],
        compiler_params=_params("parallel"),
        name="gated_merge",
    )(x, g, o_r, *att, w_in_b, w_in_b, w_in_b, w_in_b, w_pa, w_pb, w_o)


def _ffn_kernel(h_ref, g2_ref, win_ref, wout_ref, gf_ref, y_ref, t_ref, *, tc):
    h = h_ref[...]
    hn = _rms(h, g2_ref[...]).astype(BF16)
    for c in range(FFN_HIDDEN // tc):
        a = _dot(hn, win_ref[:, c * tc:(c + 1) * tc])
        b = _dot(hn, win_ref[:, FFN_HIDDEN + c * tc:FFN_HIDDEN + (c + 1) * tc])
        t_ref[:, c * tc:(c + 1) * tc] = (a * _sigmoid(a) * b).astype(BF16)
    h2 = h + _dot(t_ref[...], wout_ref[...])
    y_ref[...] = _rms(h2, gf_ref[...])


def _ffn(h, g2, w_in, w_out, gf, tm):
    m = h.shape[0]
    tok = pl.BlockSpec((tm, D_MODEL), lambda i: (i, 0))
    vec = pl.BlockSpec((1, D_MODEL), lambda i: (0, 0))
    return pl.pallas_call(
        functools.partial(_ffn_kernel, tc=256),
        out_shape=jax.ShapeDtypeStruct((m, D_MODEL), F32),
        grid=(m // tm,),
        in_specs=[tok, vec, _resident(w_in.shape), _resident(w_out.shape), vec],
        out_specs=tok,
        scratch_shapes=[pltpu.VMEM((tm, FFN_HIDDEN), BF16)],
        compiler_params=_params("parallel"),
        name="swiglu_final_norm",
    )(h, g2, w_in, w_out, gf)


def _rotary_tables(pos):
    half = RET_DK // 2
    inv = ROPE_BASE ** (-jnp.arange(half, dtype=F32) / half)
    ang = pos.astype(F32)[:, None] * inv[None, :]
    return jnp.cos(ang), jnp.sin(ang)


def kernel(x_prompt, x_sample, state_ret, cache_kv_w128, cache_kv_w512, cache_kv_w2048,
           ln1_g, w_in, ret_gn_g, w_pa, w_pb, w_o, ln2_g, w_ffn_in, w_ffn_out, lnf_g):
    batch, seq, _ = x_prompt.shape
    nsample = x_sample.shape[0]
    w_in_b = w_in[0].astype(BF16)
    w_pa_b, w_pb_b, w_o_b = w_pa[0].astype(BF16), w_pb[0].astype(BF16), w_o[0].astype(BF16)
    w_ffn_in_b, w_ffn_out_b = w_ffn_in[0].astype(BF16), w_ffn_out[0].astype(BF16)
    lnf = lnf_g.reshape(1, D_MODEL)
    gn = ret_gn_g.reshape(1, RET_V)

    def finish(x, o_r, att, tm_merge, tm_ffn, rows_per_seq):
        h = _merge(x, ln1_g, o_r, att, w_in_b, w_pa_b, w_pb_b, w_o_b, tm_merge, rows_per_seq)
        return _ffn(h, ln2_g, w_ffn_in_b, w_ffn_out_b, lnf, tm_ffn)

    xp = x_prompt.reshape(batch * seq, D_MODEL)
    cos, sin = _rotary_tables(jnp.arange(seq, dtype=jnp.int32))
    q_r, k_r, v_r, g_r = _ret_proj(xp, ln1_g, cos, sin, w_in_b, seq)
    qkv = _att_proj(xp, ln1_g, w_in_b, batch, seq)
    o_r, s_p = _ret_prompt(q_r, k_r, v_r, g_r, gn, batch, seq)
    att, kv_p = [], []
    for gi, (win, _) in enumerate(ATT_GROUPS):
        q_g, k_g, v_g = qkv[3 * gi:3 * gi + 3]
        att += list(_att_prompt(q_g, k_g, v_g, gi, batch, seq))
        kv_p.append(qkv[3 * N_GROUPS + gi].reshape(1, batch, min(win, seq), 2, ATT_HPG, ATT_HD))
    y_p = finish(xp, o_r, att, MERGE_TM, FFN_TM, seq)

    xs = x_sample.reshape(nsample, D_MODEL)
    xn_s = _rmsnorm(xs, ln1_g, nsample)
    rot = tuple(jnp.broadcast_to(t, (nsample, RET_DK // 2))
                for t in _rotary_tables(PAST_LEN + jnp.arange(1, dtype=jnp.int32)))
    q_rs = _proj(xn_s, w_in_b, OFF_QR, RET_QK, F32, nsample, rot=rot)
    k_rs = _proj(xn_s, w_in_b, OFF_KR, RET_QK, F32, nsample, rot=rot, scale=RET_DK ** -0.5)
    v_rs = _proj(xn_s, w_in_b, OFF_VR, RET_V, F32, nsample)
    g_rs = _proj(xn_s, w_in_b, OFF_GR, RET_V, F32, nsample)
    q_as = _proj(xn_s, w_in_b, OFF_QA, ATT_W, F32, nsample)
    k_as = _proj(xn_s, w_in_b, OFF_KA, ATT_W, F32, nsample)
    v_as = _proj(xn_s, w_in_b, OFF_VA, ATT_W, F32, nsample)
    col = (nsample, RET_HEADS, RET_DK, 1)
    row = (nsample, RET_HEADS, 1, RET_DV)
    o_rs, s_s = _ret_sample(q_rs.reshape(col), k_rs.reshape(col), v_rs.reshape(row), g_rs.reshape(row),
                            ret_gn_g.reshape(RET_HEADS, 1, RET_DV), state_ret[0])
    grp = (nsample, N_GROUPS, ATT_HPG, ATT_HD)
    q4 = q_as.reshape(grp)
    kv_new = jnp.concatenate([k_as.reshape(grp), v_as.reshape(grp)], axis=2)
    q8 = jnp.concatenate([q4, jnp.zeros_like(q4)], axis=2)
    caches = [c[0].reshape(nsample, c.shape[2], 2 * ATT_HPG, ATT_HD)
              for c in (cache_kv_w128, cache_kv_w512, cache_kv_w2048)]
    att_s = [jnp.swapaxes(a, 0, 1)[None] for a in _att_sample(q8, kv_new, caches)]
    y_s = finish(xs, o_rs.reshape(nsample, RET_V).astype(BF16), att_s, nsample, nsample, nsample)
    kv_s = [kv_new[:, gi].reshape(1, nsample, 1, 2, ATT_HPG, ATT_HD) for gi in range(N_GROUPS)]

    return (y_p.reshape(batch, seq, D_MODEL), y_s.reshape(nsample, 1, D_MODEL), s_p[None], s_s[None],
            kv_p[0], kv_s[0], kv_p[1], kv_s[1], kv_p[2], kv_s[2])
```
